```python
import math
import jax, jax.numpy as jnp
from jax import lax
import numpy as np

D_MODEL = 1024
BATCH = 4
SEQ = 8192
DEPTH = 2

N_HEADS = 16
HEAD_DIM = D_MODEL // N_HEADS
MOBA_BLOCK = 256
MOBA_TOPK = 3
Q_CHUNK = 32
ROPE_THETA = 10000.0
RWKV_HEAD = 64
RWKV_HEADS = D_MODEL // RWKV_HEAD
DECAY_LORA = 64
AAA_LORA = 64
GATE_LORA = 128
LNX_EPS = 64e-5
D_FF = ((8 * D_MODEL // 3 + 127) // 128) * 128
CONV_WIDTH = 3
N_MIXERS = 2
N_ATTN_LAYERS = (DEPTH + N_MIXERS - 1) // N_MIXERS
N_RWKV_LAYERS = DEPTH // N_MIXERS
RMS_EPS = 1e-6

kernel_name = 'moba_rwkv7_convffn_hybrid'


def rms_norm(x, g):
    xf = x.astype(jnp.float32)
    y = xf * lax.rsqrt(jnp.mean(xf * xf, axis=-1, keepdims=True) + RMS_EPS)
    return (y * g.astype(jnp.float32)).astype(x.dtype)


def rope_tables(seq):
    inv = 1.0 / (ROPE_THETA ** (jnp.arange(0, HEAD_DIM, 2, dtype=jnp.float32) / HEAD_DIM))
    ang = jnp.arange(seq, dtype=jnp.float32)[:, None] * inv[None, :]
    ang = jnp.concatenate([ang, ang], axis=-1)
    return jnp.cos(ang), jnp.sin(ang)


def apply_rope(t, cos, sin):
    half = HEAD_DIM // 2
    rot = jnp.concatenate([-t[..., half:], t[..., :half]], axis=-1)
    return t * cos.astype(t.dtype) + rot * sin.astype(t.dtype)


def moba_attention(x, w_qkv, w_o, cos, sin):
    B, S, D = x.shape
    H, Dh = N_HEADS, HEAD_DIM
    qkv = x @ w_qkv
    q, k, v = jnp.split(qkv, 3, axis=-1)
    to_heads = lambda t: t.reshape(B, S, H, Dh).transpose(0, 2, 1, 3)
    q = apply_rope(to_heads(q), cos, sin)
    k = apply_rope(to_heads(k), cos, sin)
    v = to_heads(v)
    scale = 1.0 / math.sqrt(Dh)

    nb = -(-S // MOBA_BLOCK)
    pad = nb * MOBA_BLOCK - S
    kb = jnp.pad(k, ((0, 0), (0, 0), (0, pad), (0, 0))).reshape(B, H, nb, MOBA_BLOCK, Dh)
    vb = jnp.pad(v, ((0, 0), (0, 0), (0, pad), (0, 0))).reshape(B, H, nb, MOBA_BLOCK, Dh)

    kmean = jnp.mean(kb.astype(jnp.float32), axis=3)
    gate = jnp.einsum('bhsd,bhnd->bhsn', q.astype(jnp.float32), kmean)
    qblk = jnp.arange(S) // MOBA_BLOCK
    past = jnp.arange(nb)[None, :] < qblk[:, None]
    gate = jnp.where(past, gate, -jnp.inf)
    k_sel = min(MOBA_TOPK, nb)
    _, gidx = lax.top_k(gate, k_sel)
    gvalid = gidx < qblk[:, None]

    bi = jnp.arange(B)[:, None, None, None]
    hi = jnp.arange(H)[None, :, None, None]

    def chunk(c):
        s0 = c * Q_CHUNK
        qc = lax.dynamic_slice_in_dim(q, s0, Q_CHUNK, axis=2)
        idc = lax.dynamic_slice_in_dim(gidx, s0, Q_CHUNK, axis=2)
        vdc = lax.dynamic_slice_in_dim(gvalid, s0, Q_CHUNK, axis=2)
        kg = kb[bi, hi, idc]
        vg = vb[bi, hi, idc]
        s_sel = jnp.einsum('bhqd,bhqjtd->bhqjt', qc, kg) * scale
        s_sel = jnp.where(vdc[..., None], s_sel, -jnp.inf)
        s_sel = s_sel.reshape(B, H, Q_CHUNK, k_sel * MOBA_BLOCK)
        own = s0 // MOBA_BLOCK
        ko = lax.dynamic_index_in_dim(kb, own, axis=2, keepdims=False)
        vo = lax.dynamic_index_in_dim(vb, own, axis=2, keepdims=False)
        s_own = jnp.einsum('bhqd,bhtd->bhqt', qc, ko) * scale
        qpos = s0 + jnp.arange(Q_CHUNK)
        kpos = own * MOBA_BLOCK + jnp.arange(MOBA_BLOCK)
        s_own = jnp.where(kpos[None, :] <= qpos[:, None], s_own, -jnp.inf)
        s_all = jnp.concatenate([s_sel, s_own], axis=-1).astype(jnp.float32)
        p = jax.nn.softmax(s_all, axis=-1).astype(v.dtype)
        p_sel = p[..., :k_sel * MOBA_BLOCK].reshape(B, H, Q_CHUNK, k_sel, MOBA_BLOCK)
        p_own = p[..., k_sel * MOBA_BLOCK:]
        return (jnp.einsum('bhqjt,bhqjtd->bhqd', p_sel, vg)
                + jnp.einsum('bhqt,bhtd->bhqd', p_own, vo))

    out = lax.map(chunk, jnp.arange(S // Q_CHUNK))
    out = out.transpose(1, 0, 3, 2, 4).reshape(B, S, D)
    return out @ w_o


def rwkv7_time_mix(x, mu, w_rkv, w0, w1, w2, a0, a1, a2, g1, g2, k_k, k_a, r_k,
                   lnx_w, lnx_b, w_o):
    B, S, D = x.shape
    H, N = RWKV_HEADS, RWKV_HEAD
    f32 = jnp.float32
    x_prev = jnp.pad(x[:, :-1], ((0, 0), (1, 0), (0, 0)))
    xx = x_prev - x
    xr, xw, xk, xv, xa, xg = [x + xx * mu[n] for n in range(6)]
    rkv = jnp.einsum('nbsd,nde->nbse', jnp.stack([xr, xk, xv]), w_rkv)
    r, k, v = rkv[0], rkv[1], rkv[2]
    w = -jax.nn.softplus(-(w0 + jnp.tanh(xw @ w1) @ w2).astype(f32)) - 0.5
    decay = jnp.exp(-jnp.exp(w))
    a = jax.nn.sigmoid((a0 + (xa @ a1) @ a2).astype(f32))
    g = jax.nn.sigmoid(xg @ g1) @ g2

    heads = lambda t: t.astype(f32).reshape(B, S, H, N)
    r, k, v, decay, a = heads(r), heads(k), heads(v), heads(decay), heads(a)
    kk = k * k_k.astype(f32).reshape(H, N)
    kk = kk / jnp.maximum(jnp.sqrt(jnp.sum(kk * kk, axis=-1, keepdims=True)), 1e-12)
    k = k * (1.0 + (a - 1.0) * k_a.astype(f32).reshape(H, N))

    def step(state, inp):
        r_t, w_t, k_t, v_t, kk_t, a_t = inp
        sa = jnp.einsum('bhvk,bhk->bhv', state, -kk_t)
        state = (state * w_t[:, :, None, :]
                 + sa[..., None] * (kk_t * a_t)[:, :, None, :]
                 + v_t[..., None] * k_t[:, :, None, :])
        y = jnp.einsum('bhvk,bhk->bhv', state, r_t)
        return state, y

    tm = lambda t: jnp.swapaxes(t, 0, 1)
    _, y = lax.scan(step, jnp.zeros((B, H, N, N), f32),
                    (tm(r), tm(decay), tm(k), tm(v), tm(kk), tm(a)))
    y = jnp.swapaxes(y, 0, 1)
    mean = jnp.mean(y, axis=-1, keepdims=True)
    var = jnp.mean(jnp.square(y - mean), axis=-1, keepdims=True)
    yn = ((y - mean) * lax.rsqrt(var + LNX_EPS)).reshape(B, S, D)
    yn = yn * lnx_w.astype(f32) + lnx_b.astype(f32)
    bonus = (jnp.sum(r * k * r_k.astype(f32), axis=-1, keepdims=True) * v).reshape(B, S, D)
    return ((yn + bonus).astype(x.dtype) * g) @ w_o


def conv_ffn(x, w_gate, w_up, conv_w, conv_b, w_down):
    gt = x @ w_gate
    gt = lax.conv_general_dilated(
        gt, conv_w[:, None, :].astype(gt.dtype), window_strides=(1,),
        padding=[(CONV_WIDTH - 1, 0)], dimension_numbers=('NWC', 'WIO', 'NWC'),
        feature_group_count=gt.shape[-1]) + conv_b
    return (jax.nn.silu(gt) * (x @ w_up)) @ w_down


def setup_inputs(seed: int = 0) -> dict:
    key = jax.random.key(seed)
    ks = iter(jax.random.split(key, 40))
    nrm = lambda shape, s: jax.random.normal(next(ks), shape, jnp.float32) * s
    D, F, H, N = D_MODEL, D_FF, RWKV_HEADS, RWKV_HEAD
    NA, NR = N_ATTN_LAYERS, N_RWKV_LAYERS
    return {
        'x': nrm((BATCH, SEQ, D), 1.0),
        'norm_mix': 1.0 + nrm((DEPTH, D), 0.02),
        'norm_ffn': 1.0 + nrm((DEPTH, D), 0.02),
        'norm_final': 1.0 + nrm((D,), 0.02),
        'attn_w_qkv': nrm((NA, D, 3 * D), D ** -0.5),
        'attn_w_o': nrm((NA, D, D), D ** -0.5),
        'rwkv_mu': jax.random.uniform(next(ks), (NR, 6, D), jnp.float32),
        'rwkv_w_rkv': nrm((NR, 3, D, D), D ** -0.5),
        'rwkv_w0': -1.0 + nrm((NR, D), 0.5),
        'rwkv_w1': nrm((NR, D, DECAY_LORA), D ** -0.5),
        'rwkv_w2': nrm((NR, DECAY_LORA, D), 0.5 * DECAY_LORA ** -0.5),
        'rwkv_a0': nrm((NR, D), 0.1),
        'rwkv_a1': nrm((NR, D, AAA_LORA), D ** -0.5),
        'rwkv_a2': nrm((NR, AAA_LORA, D), 0.5 * AAA_LORA ** -0.5),
        'rwkv_g1': nrm((NR, D, GATE_LORA), D ** -0.5),
        'rwkv_g2': nrm((NR, GATE_LORA, D), GATE_LORA ** -0.5),
        'rwkv_k_k': 0.85 + nrm((NR, D), 0.05),
        'rwkv_k_a': 1.0 + nrm((NR, D), 0.05),
        'rwkv_r_k': nrm((NR, H, N), 0.1),
        'rwkv_lnx_w': 1.0 + nrm((NR, D), 0.02),
        'rwkv_lnx_b': nrm((NR, D), 0.02),
        'rwkv_w_o': nrm((NR, D, D), D ** -0.5),
        'ffn_w_gate': nrm((DEPTH, D, F), D ** -0.5),
        'ffn_w_up': nrm((DEPTH, D, F), D ** -0.5),
        'ffn_conv_w': nrm((DEPTH, CONV_WIDTH, F), CONV_WIDTH ** -0.5),
        'ffn_conv_b': nrm((DEPTH, F), 0.02),
        'ffn_w_down': nrm((DEPTH, F, D), F ** -0.5),
    }


def reference(x, norm_mix, norm_ffn, norm_final, attn_w_qkv, attn_w_o,
              rwkv_mu, rwkv_w_rkv, rwkv_w0, rwkv_w1, rwkv_w2, rwkv_a0, rwkv_a1,
              rwkv_a2, rwkv_g1, rwkv_g2, rwkv_k_k, rwkv_k_a, rwkv_r_k,
              rwkv_lnx_w, rwkv_lnx_b, rwkv_w_o,
              ffn_w_gate, ffn_w_up, ffn_conv_w, ffn_conv_b, ffn_w_down):
    S = x.shape[1]
    cos, sin = rope_tables(S)
    h = x
    for i in range(DEPTH):
        xn = rms_norm(h, norm_mix[i])
        j = i // N_MIXERS
        if i % N_MIXERS == 0:
            h = h + moba_attention(xn, attn_w_qkv[j], attn_w_o[j], cos, sin)
        else:
            h = h + rwkv7_time_mix(
                xn, rwkv_mu[j], rwkv_w_rkv[j], rwkv_w0[j], rwkv_w1[j], rwkv_w2[j],
                rwkv_a0[j], rwkv_a1[j], rwkv_a2[j], rwkv_g1[j], rwkv_g2[j],
                rwkv_k_k[j], rwkv_k_a[j], rwkv_r_k[j], rwkv_lnx_w[j], rwkv_lnx_b[j],
                rwkv_w_o[j])
        h = h + conv_ffn(rms_norm(h, norm_ffn[i]), ffn_w_gate[i], ffn_w_up[i],
                         ffn_conv_w[i], ffn_conv_b[i], ffn_w_down[i])
    return rms_norm(h, norm_final)
```

```python
import functools
import math

import jax
import jax.numpy as jnp
from jax import lax
from jax.experimental import pallas as pl
from jax.experimental.pallas import tpu as pltpu

F32 = jnp.float32
BF16 = jnp.bfloat16
HIGHEST = lax.Precision.HIGHEST

LANES = 128
HEAD_DIM = 64
HEADS_PER_PAIR = LANES // HEAD_DIM
MOBA_BLOCK = 256
MOBA_TOPK = 3
ROPE_THETA = 10000.0
RMS_EPS = 1e-6
LNX_EPS = 64e-5
CONV_WIDTH = 3
RWKV_CHUNK = 64
RWKV_STEP = 256
FFN_FCHUNK = 256
TOKEN_TILE = 512
HALO_F32 = 8
HALO_BF16 = 16
VMEM_LIMIT = 56 * 1024 * 1024

_NT = (((1,), (1,)), ((), ()))
_TN = (((0,), (0,)), ((), ()))


def _mm(a, b):
    return jnp.dot(a.astype(BF16), b.astype(BF16), preferred_element_type=F32)


def _mm_nt(a, b):
    return lax.dot_general(a.astype(BF16), b.astype(BF16), _NT, preferred_element_type=F32)


def _mm_tn(a, b):
    return lax.dot_general(a.astype(BF16), b.astype(BF16), _TN, preferred_element_type=F32)


def _rms(x, g):
    y = x * lax.rsqrt(jnp.mean(x * x, axis=-1, keepdims=True) + RMS_EPS)
    return y * g


def _params(n_axes):
    return pltpu.CompilerParams(
        dimension_semantics=("arbitrary",) * n_axes, vmem_limit_bytes=VMEM_LIMIT)


def _const_spec(shape):
    zeros = (0,) * len(shape)
    return pl.BlockSpec(shape, lambda *_: zeros)


def _qkv_kernel(x_ref, g_ref, wq_ref, wk_ref, wvt_ref, cos_ref, sin_ref,
                q_ref, k_ref, vt_ref, km_ref, *, scale):
    ts, d = x_ref.shape[1], x_ref.shape[2]
    xb = _rms(x_ref[0], g_ref[...]).astype(BF16)
    q = jnp.dot(xb, wq_ref[...], preferred_element_type=F32)
    k = jnp.dot(xb, wk_ref[...], preferred_element_type=F32)
    vt = lax.dot_general(wvt_ref[...], xb, _NT, preferred_element_type=F32)

    reps = d // LANES
    cos = jnp.concatenate([cos_ref[...]] * reps, axis=1)
    sin = jnp.concatenate([sin_ref[...]] * reps, axis=1)
    lane = lax.broadcasted_iota(jnp.int32, (1, d), 1)
    first_half = (lane & (HEAD_DIM - 1)) < HEAD_DIM // 2

    def rope(t):
        partner = jnp.where(first_half,
                            pltpu.roll(t, d - HEAD_DIM // 2, 1),
                            pltpu.roll(t, HEAD_DIM // 2, 1))
        return t * cos + partner * sin

    q = rope(q) * scale
    k = rope(k)
    q_ref[0] = q.astype(BF16)
    k_ref[0] = k.astype(BF16)
    for r in range(ts // MOBA_BLOCK):
        rows = slice(r * MOBA_BLOCK, (r + 1) * MOBA_BLOCK)
        vt_ref[0, r] = vt[:, rows].astype(BF16)
        km_ref[0, r] = jnp.mean(k[rows], axis=0, keepdims=True)


def _qkv_call(x, g, wq, wk, wvt, cos, sin):
    b, s, d = x.shape
    ts = TOKEN_TILE
    nb = s // MOBA_BLOCK
    bpt = ts // MOBA_BLOCK
    scale = 1.0 / math.sqrt(HEAD_DIM)
    return pl.pallas_call(
        functools.partial(_qkv_kernel, scale=scale),
        grid=(b, s // ts),
        in_specs=[
            pl.BlockSpec((1, ts, d), lambda bi, i: (bi, i, 0)),
            _const_spec((1, d)),
            _const_spec((d, d)),
            _const_spec((d, d)),
            _const_spec((d, d)),
            pl.BlockSpec((ts, LANES), lambda bi, i: (i, 0)),
            pl.BlockSpec((ts, LANES), lambda bi, i: (i, 0)),
        ],
        out_specs=[
            pl.BlockSpec((1, ts, d), lambda bi, i: (bi, i, 0)),
            pl.BlockSpec((1, ts, d), lambda bi, i: (bi, i, 0)),
            pl.BlockSpec((1, bpt, d, MOBA_BLOCK), lambda bi, i: (bi, i, 0, 0)),
            pl.BlockSpec((1, bpt, 1, d), lambda bi, i: (bi, i, 0, 0)),
        ],
        out_shape=[
            jax.ShapeDtypeStruct((b, s, d), BF16),
            jax.ShapeDtypeStruct((b, s, d), BF16),
            jax.ShapeDtypeStruct((b, nb, d, MOBA_BLOCK), BF16),
            jax.ShapeDtypeStruct((b, nb, 1, d), F32),
        ],
        compiler_params=_params(2),
        name="qkv_rope",
    )(x, g, wq, wk, wvt, cos, sin)


def _moba_kernel(q_ref, k_ref, vt_ref, km_ref, o_ref, bias_ref):
    i = pl.program_id(2)
    blk = MOBA_BLOCK
    nb = km_ref.shape[1]
    neg_inf = -jnp.inf
    q = q_ref[0]
    km = km_ref[0, :, 0, :]
    lane = lax.broadcasted_iota(jnp.int32, (1, LANES), 1)
    jrow = lax.broadcasted_iota(jnp.int32, (nb, blk), 0)
    past = jrow < i

    qms = []
    for hh in range(HEADS_PER_PAIR):
        in_head = (lane >= hh * HEAD_DIM) & (lane < (hh + 1) * HEAD_DIM)
        qm = jnp.where(in_head, q, jnp.zeros_like(q))
        qms.append(qm)
        kmm = jnp.where(in_head, km, 0.0)
        gate = lax.dot_general(kmm, qm.astype(F32), _NT, precision=HIGHEST,
                               preferred_element_type=F32)
        g = jnp.where(past, gate, neg_inf)
        sel = jnp.zeros((nb, blk), jnp.bool_)
        for _ in range(MOBA_TOPK):
            m = jnp.max(g, axis=0, keepdims=True)
            idx = jnp.min(jnp.where(g == m, jrow, nb), axis=0, keepdims=True)
            pick = jrow == idx
            sel = sel | pick
            g = jnp.where(pick, neg_inf, g)
        bias_ref[hh] = jnp.where(sel & past, 0.0, neg_inf)

    krow = lax.broadcasted_iota(jnp.int32, (blk, blk), 0)
    qcol = lax.broadcasted_iota(jnp.int32, (blk, blk), 1)
    causal = krow <= qcol
    k_own = k_ref[0, pl.ds(pl.multiple_of(i * blk, blk), blk), :]
    state = []
    for hh in range(HEADS_PER_PAIR):
        s = lax.dot_general(k_own, qms[hh], _NT, preferred_element_type=F32)
        s = jnp.where(causal, s, neg_inf)
        m = jnp.max(s, axis=0, keepdims=True)
        p = jnp.exp(s - m)
        l = jnp.sum(p, axis=0, keepdims=True)
        vt = vt_ref[0, i, hh * HEAD_DIM:(hh + 1) * HEAD_DIM, :]
        acc = jnp.dot(vt, p.astype(BF16), preferred_element_type=F32)
        state += [m, l, acc]

    def body(j, carry):
        kj = k_ref[0, pl.ds(pl.multiple_of(j * blk, blk), blk), :]
        out = []
        for hh in range(HEADS_PER_PAIR):
            m, l, acc = carry[3 * hh:3 * hh + 3]
            s = lax.dot_general(kj, qms[hh], _NT, preferred_element_type=F32)
            s = s + bias_ref[hh, pl.ds(j, 1), :]
            m_new = jnp.maximum(m, jnp.max(s, axis=0, keepdims=True))
            alpha = jnp.exp(m - m_new)
            p = jnp.exp(s - m_new)
            l = alpha * l + jnp.sum(p, axis=0, keepdims=True)
            vt = vt_ref[0, j, hh * HEAD_DIM:(hh + 1) * HEAD_DIM, :]
            acc = alpha * acc + jnp.dot(vt, p.astype(BF16), preferred_element_type=F32)
            out += [m_new, l, acc]
        return tuple(out)

    state = lax.fori_loop(0, i, body, tuple(state))
    outs = [state[3 * hh + 2] / state[3 * hh + 1] for hh in range(HEADS_PER_PAIR)]
    o_ref[0] = jnp.concatenate(outs, axis=0).T.astype(o_ref.dtype)


def _moba_call(q, k, vt, km):
    b, s, d = q.shape
    nb = s // MOBA_BLOCK
    return pl.pallas_call(
        _moba_kernel,
        grid=(b, d // LANES, nb),
        in_specs=[
            pl.BlockSpec((1, MOBA_BLOCK, LANES), lambda bi, hp, i: (bi, i, hp)),
            pl.BlockSpec((1, s, LANES), lambda bi, hp, i: (bi, 0, hp)),
            pl.BlockSpec((1, nb, LANES, MOBA_BLOCK), lambda bi, hp, i: (bi, 0, hp, 0)),
            pl.BlockSpec((1, nb, 1, LANES), lambda bi, hp, i: (bi, 0, 0, hp)),
        ],
        out_specs=pl.BlockSpec((1, MOBA_BLOCK, LANES), lambda bi, hp, i: (bi, i, hp)),
        out_shape=jax.ShapeDtypeStruct((b, s, d), BF16),
        scratch_shapes=[pltpu.VMEM((HEADS_PER_PAIR, nb, MOBA_BLOCK), F32)],
        compiler_params=_params(3),
        name="moba_attention",
    )(q, k, vt, km)


def _oproj_kernel(*refs, gated):
    if gated:
        h_ref, y_ref, g_ref, w_ref, o_ref = refs
        y = y_ref[0].astype(F32) * g_ref[0]
    else:
        h_ref, y_ref, w_ref, o_ref = refs
        y = y_ref[0]
    o_ref[0] = h_ref[0] + jnp.dot(y.astype(BF16), w_ref[...], preferred_element_type=F32)


def _oproj_call(h, y, gate, w):
    b, s, d = h.shape
    ts = TOKEN_TILE
    tile = pl.BlockSpec((1, ts, d), lambda bi, i: (bi, i, 0))
    gated = gate is not None
    args = (h, y, gate, w) if gated else (h, y, w)
    return pl.pallas_call(
        functools.partial(_oproj_kernel, gated=gated),
        grid=(b, s // ts),
        in_specs=[tile] * (len(args) - 1) + [_const_spec((d, d))],
        out_specs=tile,
        out_shape=jax.ShapeDtypeStruct((b, s, d), F32),
        compiler_params=_params(2),
        name="out_proj_gated" if gated else "out_proj",
    )(*args)


def _ffn_kernel(*refs, final):
    if final:
        (x_ref, halo_ref, g_ref, wg_ref, wu_ref, cw_ref, cb_ref, wd_ref, gf_ref,
         o_ref, hn_ref, gx_ref, acc_ref) = refs
    else:
        (x_ref, halo_ref, g_ref, wg_ref, wu_ref, cw_ref, cb_ref, wd_ref,
         o_ref, hn_ref, gx_ref, acc_ref) = refs
    i = pl.program_id(1)
    ts = x_ref.shape[1]
    halo = HALO_BF16
    x = x_ref[0]
    g = g_ref[...]
    hn_ref[0:halo, :] = _rms(halo_ref[0], g).astype(BF16)
    hn_ref[halo:, :] = _rms(x, g).astype(BF16)
    acc_ref[...] = jnp.zeros_like(acc_ref)

    def chunk(c, _):
        gx_ref[...] = jnp.dot(hn_ref[...], wg_ref[c], preferred_element_type=F32)

        @pl.when(i == 0)
        def _():
            gx_ref[0:halo, :] = jnp.zeros((halo, gx_ref.shape[1]), F32)

        cw = cw_ref[c]
        conv = (gx_ref[halo - 2:halo - 2 + ts, :] * cw[0:1]
                + gx_ref[halo - 1:halo - 1 + ts, :] * cw[1:2]
                + gx_ref[halo:halo + ts, :] * cw[2:3]
                + cb_ref[c])
        up = jnp.dot(hn_ref[halo:, :], wu_ref[c], preferred_element_type=F32)
        act = (conv * jax.nn.sigmoid(conv) * up).astype(BF16)
        acc_ref[...] += jnp.dot(act, wd_ref[c], preferred_element_type=F32)
        return 0

    lax.fori_loop(0, wg_ref.shape[0], chunk, 0)
    out = x + acc_ref[...]
    if final:
        out = _rms(out, gf_ref[...])
    o_ref[0] = out


def _ffn_call(h, g, wg, wu, cw, cb, wd, g_final):
    b, s, d = h.shape
    ts = TOKEN_TILE
    nc, _, fc = wg.shape
    halo = HALO_BF16
    hpt = ts // halo
    final = g_final is not None
    tile = pl.BlockSpec((1, ts, d), lambda bi, i: (bi, i, 0))
    in_specs = [
        tile,
        pl.BlockSpec((1, halo, d), lambda bi, i: (bi, jnp.maximum(i * hpt - 1, 0), 0)),
        _const_spec((1, d)),
        _const_spec((nc, d, fc)),
        _const_spec((nc, d, fc)),
        _const_spec((nc, CONV_WIDTH, fc)),
        _const_spec((nc, 1, fc)),
        _const_spec((nc, fc, d)),
    ]
    args = [h, h, g, wg, wu, cw, cb, wd]
    if final:
        in_specs.append(_const_spec((1, d)))
        args.append(g_final)
    return pl.pallas_call(
        functools.partial(_ffn_kernel, final=final),
        grid=(b, s // ts),
        in_specs=in_specs,
        out_specs=tile,
        out_shape=jax.ShapeDtypeStruct((b, s, d), F32),
        scratch_shapes=[
            pltpu.VMEM((ts + halo, d), BF16),
            pltpu.VMEM((ts + halo, fc), F32),
            pltpu.VMEM((ts, d), F32),
        ],
        compiler_params=_params(2),
        name="conv_ffn_final" if final else "conv_ffn",
    )(*args)


def _rwkv_proj_kernel(x_ref, halo_ref, gm_ref, mu_ref, wr_ref, wk_ref, wv_ref,
                      w0_ref, w1_ref, w2_ref, a0_ref, a1_ref, a2_ref, g1_ref, g2_ref,
                      r_ref, k_ref, v_ref, lw_ref, a_ref, g_ref, xs_ref):
    i = pl.program_id(1)
    ts = x_ref.shape[1]
    halo = HALO_F32
    gm = gm_ref[...]
    xn = _rms(x_ref[0], gm)
    xs_ref[0:halo, :] = _rms(halo_ref[0], gm)
    xs_ref[halo:, :] = xn

    @pl.when(i == 0)
    def _():
        xs_ref[0:halo, :] = jnp.zeros((halo, xs_ref.shape[1]), F32)

    xx = xs_ref[halo - 1:halo - 1 + ts, :] - xn

    def mix(n):
        return (xn + xx * mu_ref[n:n + 1, :]).astype(BF16)

    r_ref[0] = jnp.dot(mix(0), wr_ref[...], preferred_element_type=F32)
    k_ref[0] = jnp.dot(mix(2), wk_ref[...], preferred_element_type=F32)
    v_ref[0] = jnp.dot(mix(3), wv_ref[...], preferred_element_type=F32)

    wl = w0_ref[...] + _mm(jnp.tanh(jnp.dot(mix(1), w1_ref[...], preferred_element_type=F32)),
                           w2_ref[...])
    z = -wl
    softplus = jnp.maximum(z, 0.0) + jnp.log(1.0 + jnp.exp(-jnp.abs(z)))
    w = -softplus - 0.5
    lw_ref[0] = -jnp.exp(w)

    al = a0_ref[...] + _mm(jnp.dot(mix(4), a1_ref[...], preferred_element_type=F32), a2_ref[...])
    a_ref[0] = jax.nn.sigmoid(al)
    g_ref[0] = _mm(jax.nn.sigmoid(jnp.dot(mix(5), g1_ref[...], preferred_element_type=F32)),
                   g2_ref[...])


def _rwkv_proj_call(h, gm, mu, wr, wk, wv, w0, w1, w2, a0, a1, a2, g1, g2):
    b, s, d = h.shape
    ts = RWKV_STEP
    halo = HALO_F32
    hpt = ts // halo
    tile = pl.BlockSpec((1, ts, d), lambda bi, i: (bi, i, 0))
    consts = [gm, mu, wr, wk, wv, w0, w1, w2, a0, a1, a2, g1, g2]
    return pl.pallas_call(
        _rwkv_proj_kernel,
        grid=(b, s // ts),
        in_specs=[tile,
                  pl.BlockSpec((1, halo, d), lambda bi, i: (bi, jnp.maximum(i * hpt - 1, 0), 0))]
                 + [_const_spec(c.shape) for c in consts],
        out_specs=[tile] * 6,
        out_shape=[jax.ShapeDtypeStruct((b, s, d), F32)] * 6,
        scratch_shapes=[pltpu.VMEM((ts + halo, d), F32)],
        compiler_params=_params(2),
        name="rwkv_proj",
    )(h, h, *consts)


def _rwkv_chunk(r, k, v, lw, a, k_k, k_a, r_k, ln_w, ln_b, h0, masks):
    ltri, strict, incl, eye = masks
    c = r.shape[0]
    cum = jnp.dot(ltri, lw, precision=HIGHEST, preferred_element_type=F32)
    e_w = jnp.exp(cum)
    e_wi = jnp.exp(-cum)
    e_wp = jnp.exp(cum - lw)
    kk = k * k_k
    kk = kk / jnp.maximum(jnp.sqrt(jnp.sum(kk * kk, axis=-1, keepdims=True)), 1e-12)
    kmod = k * (1.0 + (a - 1.0) * k_a)
    a_t = -kk * e_wp
    b_t = kk * a * e_wi
    k_t = kmod * e_wi
    r_t = r * e_w
    w_c = e_w[c - 1:c, :]
    b_h = b_t * w_c
    k_h = k_t * w_c

    zero = jnp.zeros((c, c), F32)
    n_ab = jnp.where(strict, _mm_nt(a_t, b_t), zero)
    m_ak = jnp.where(strict, _mm_nt(a_t, k_t), zero)
    m_rb = jnp.where(incl, _mm_nt(r_t, b_t), zero)
    m_rk = jnp.where(incl, _mm_nt(r_t, k_t), zero)

    inv = jnp.where(eye, 1.0, 0.0) + n_ab
    p = n_ab
    for _ in range(int(math.log2(c)) - 1):
        p = _mm(p, p)
        inv = inv + _mm(inv, p)

    u0 = _mm(inv, _mm(m_ak, v))
    a2 = _mm(inv, a_t)
    r_p = r_t + _mm(m_rb, a2)
    y0 = _mm(m_rb, u0) + _mm(m_rk, v)
    g_m = jnp.where(eye, w_c, 0.0) + _mm_tn(b_h, a2)
    d_m = _mm_tn(b_h, u0) + _mm_tn(k_h, v)

    y = _mm(r_p, h0) + y0
    h1 = _mm(g_m, h0) + d_m

    mean = jnp.mean(y, axis=-1, keepdims=True)
    var = jnp.mean(jnp.square(y - mean), axis=-1, keepdims=True)
    yn = (y - mean) * lax.rsqrt(var + LNX_EPS) * ln_w + ln_b
    bonus = jnp.sum(r * kmod * r_k, axis=-1, keepdims=True) * v
    return yn + bonus, h1


def _rwkv_core_kernel(r_ref, k_ref, v_ref, lw_ref, a_ref, kk_ref, ka_ref, rk_ref,
                      lnw_ref, lnb_ref, o_ref, h_ref):
    c = RWKV_CHUNK
    step = r_ref.shape[1]

    @pl.when(pl.program_id(2) == 0)
    def _():
        h_ref[...] = jnp.zeros_like(h_ref)

    row = lax.broadcasted_iota(jnp.int32, (c, c), 0)
    col = lax.broadcasted_iota(jnp.int32, (c, c), 1)
    incl = row >= col
    masks = (jnp.where(incl, 1.0, 0.0), row > col, incl, row == col)

    outs = []
    for hh in range(HEADS_PER_PAIR):
        lanes = slice(hh * HEAD_DIM, (hh + 1) * HEAD_DIM)
        par = [ref[...][:, lanes] for ref in (kk_ref, ka_ref, rk_ref, lnw_ref, lnb_ref)]
        h = h_ref[hh]
        rows_out = []
        for ch in range(step // c):
            rows = slice(ch * c, (ch + 1) * c)
            ins = [ref[0, rows, :][:, lanes] for ref in (r_ref, k_ref, v_ref, lw_ref, a_ref)]
            o, h = _rwkv_chunk(*ins, *par, h, masks)
            rows_out.append(o)
        h_ref[hh] = h
        outs.append(jnp.concatenate(rows_out, axis=0))
    o_ref[0] = jnp.concatenate(outs, axis=1)


def _rwkv_core_call(r, k, v, lw, a, k_k, k_a, r_k, ln_w, ln_b):
    b, s, d = r.shape
    step = RWKV_STEP
    tile = pl.BlockSpec((1, step, LANES), lambda bi, hp, i: (bi, i, hp))
    par = pl.BlockSpec((1, LANES), lambda bi, hp, i: (0, hp))
    return pl.pallas_call(
        _rwkv_core_kernel,
        grid=(b, d // LANES, s // step),
        in_specs=[tile] * 5 + [par] * 5,
        out_specs=tile,
        out_shape=jax.ShapeDtypeStruct((b, s, d), F32),
        scratch_shapes=[pltpu.VMEM((HEADS_PER_PAIR, HEAD_DIM, HEAD_DIM), F32)],
        compiler_params=_params(3),
        name="rwkv_recurrence",
    )(r, k, v, lw, a, k_k, k_a, r_k, ln_w, ln_b)


def _rope_tables(seq):
    inv = 1.0 / (ROPE_THETA ** (jnp.arange(0, HEAD_DIM, 2, dtype=F32) / HEAD_DIM))
    ang = jnp.arange(seq, dtype=F32)[:, None] * inv[None, :]
    cos, sin = jnp.cos(ang), jnp.sin(ang)
    cos = jnp.concatenate([cos, cos] * HEADS_PER_PAIR, axis=-1)
    sin = jnp.concatenate([-sin, sin] * HEADS_PER_PAIR, axis=-1)
    return cos, sin


def _ffn_weights(w_gate, w_up, conv_w, conv_b, w_down):
    d, f = w_gate.shape
    fc = FFN_FCHUNK
    nc = f // fc
    wg = w_gate.astype(BF16).reshape(d, nc, fc).transpose(1, 0, 2)
    wu = w_up.astype(BF16).reshape(d, nc, fc).transpose(1, 0, 2)
    cw = conv_w.reshape(CONV_WIDTH, nc, fc).transpose(1, 0, 2)
    cb = conv_b.reshape(nc, 1, fc)
    wd = w_down.astype(BF16).reshape(nc, fc, d)
    return wg, wu, cw, cb, wd


def kernel(x, norm_mix, norm_ffn, norm_final, attn_w_qkv, attn_w_o, rwkv_mu, rwkv_w_rkv, rwkv_w0, rwkv_w1, rwkv_w2, rwkv_a0, rwkv_a1, rwkv_a2, rwkv_g1, rwkv_g2, rwkv_k_k, rwkv_k_a, rwkv_r_k, rwkv_lnx_w, rwkv_lnx_b, rwkv_w_o, ffn_w_gate, ffn_w_up, ffn_conv_w, ffn_conv_b, ffn_w_down):
    b, s, d = x.shape
    depth = norm_mix.shape[0]
    n_mixers = 2
    row = lambda t: t.reshape(1, d)
    cos, sin = _rope_tables(s)
    h = x
    for i in range(depth):
        j = i // n_mixers
        if i % n_mixers == 0:
            w_qkv = attn_w_qkv[j].astype(BF16)
            q, k, vt, km = _qkv_call(h, row(norm_mix[i]), w_qkv[:, :d], w_qkv[:, d:2 * d],
                                     w_qkv[:, 2 * d:].T, cos, sin)
            attn = _moba_call(q, k, vt, km)
            h = _oproj_call(h, attn, None, attn_w_o[j].astype(BF16))
        else:
            bf = lambda t: t.astype(BF16)
            r, k, v, lw, a, g = _rwkv_proj_call(
                h, row(norm_mix[i]), rwkv_mu[j], bf(rwkv_w_rkv[j, 0]), bf(rwkv_w_rkv[j, 1]),
                bf(rwkv_w_rkv[j, 2]), row(rwkv_w0[j]), bf(rwkv_w1[j]), bf(rwkv_w2[j]),
                row(rwkv_a0[j]), bf(rwkv_a1[j]), bf(rwkv_a2[j]), bf(rwkv_g1[j]), bf(rwkv_g2[j]))
            y = _rwkv_core_call(r, k, v, lw, a, row(rwkv_k_k[j]), row(rwkv_k_a[j]),
                                row(rwkv_r_k[j]), row(rwkv_lnx_w[j]), row(rwkv_lnx_b[j]))
            h = _oproj_call(h, y, g, bf(rwkv_w_o[j]))
        g_final = row(norm_final) if i == depth - 1 else None
        h = _ffn_call(h, row(norm_ffn[i]),
                      *_ffn_weights(ffn_w_gate[i], ffn_w_up[i], ffn_conv_w[i], ffn_conv_b[i],
                                    ffn_w_down[i]), g_final)
    return h
```

```python
import functools
import math

import jax
import jax.numpy as jnp
from jax import lax
from jax.experimental import pallas as pl
from jax.experimental.pallas import tpu as pltpu

F32 = jnp.float32
BF16 = jnp.bfloat16
HIGHEST = lax.Precision.HIGHEST

LANES = 128
HEAD_DIM = 64
HEADS_PER_PAIR = LANES // HEAD_DIM
MOBA_BLOCK = 256
MOBA_TOPK = 3
MOBA_GROUP = 4
ROPE_THETA = 10000.0
RMS_EPS = 1e-6
LNX_EPS = 64e-5
CONV_WIDTH = 3
RWKV_CHUNK = 64
RWKV_STEP = 256
FFN_FCHUNK = 256
TOKEN_TILE = 512
HALO_F32 = 8
HALO_BF16 = 16
VMEM_LIMIT = 56 * 1024 * 1024

_NT = (((1,), (1,)), ((), ()))
_TN = (((0,), (0,)), ((), ()))


def _mm(a, b):
    return jnp.dot(a.astype(BF16), b.astype(BF16), preferred_element_type=F32)


def _mm_nt(a, b):
    return lax.dot_general(a.astype(BF16), b.astype(BF16), _NT, preferred_element_type=F32)


def _mm_tn(a, b):
    return lax.dot_general(a.astype(BF16), b.astype(BF16), _TN, preferred_element_type=F32)


def _rms(x, g):
    y = x * lax.rsqrt(jnp.mean(x * x, axis=-1, keepdims=True) + RMS_EPS)
    return y * g


def _params(n_axes):
    return pltpu.CompilerParams(
        dimension_semantics=("arbitrary",) * n_axes, vmem_limit_bytes=VMEM_LIMIT)


def _const_spec(shape):
    zeros = (0,) * len(shape)
    return pl.BlockSpec(shape, lambda *_: zeros)


def _qkv_kernel(x_ref, g_ref, wq_ref, wk_ref, wvt_ref, cos_ref, sin_ref,
                q_ref, k_ref, vt_ref, km_ref, *, scale):
    ts, d = x_ref.shape[1], x_ref.shape[2]
    xb = _rms(x_ref[0], g_ref[...]).astype(BF16)
    q = jnp.dot(xb, wq_ref[...], preferred_element_type=F32)
    k = jnp.dot(xb, wk_ref[...], preferred_element_type=F32)
    vt = lax.dot_general(wvt_ref[...], xb, _NT, preferred_element_type=F32)

    reps = d // LANES
    cos = jnp.concatenate([cos_ref[...]] * reps, axis=1)
    sin = jnp.concatenate([sin_ref[...]] * reps, axis=1)
    lane = lax.broadcasted_iota(jnp.int32, (1, d), 1)
    first_half = (lane & (HEAD_DIM - 1)) < HEAD_DIM // 2

    def rope(t):
        partner = jnp.where(first_half,
                            pltpu.roll(t, d - HEAD_DIM // 2, 1),
                            pltpu.roll(t, HEAD_DIM // 2, 1))
        return t * cos + partner * sin

    q = rope(q) * scale
    k = rope(k)
    q_ref[0] = q.astype(BF16)
    k_ref[0] = k.astype(BF16)
    for r in range(ts // MOBA_BLOCK):
        rows = slice(r * MOBA_BLOCK, (r + 1) * MOBA_BLOCK)
        vt_ref[0, r] = vt[:, rows].astype(BF16)
        km_ref[0, r] = jnp.mean(k[rows], axis=0, keepdims=True)


def _qkv_call(x, g, wq, wk, wvt, cos, sin):
    b, s, d = x.shape
    ts = TOKEN_TILE
    nb = s // MOBA_BLOCK
    bpt = ts // MOBA_BLOCK
    scale = math.log2(math.e) / math.sqrt(HEAD_DIM)
    return pl.pallas_call(
        functools.partial(_qkv_kernel, scale=scale),
        grid=(b, s // ts),
        in_specs=[
            pl.BlockSpec((1, ts, d), lambda bi, i: (bi, i, 0)),
            _const_spec((1, d)),
            _const_spec((d, d)),
            _const_spec((d, d)),
            _const_spec((d, d)),
            pl.BlockSpec((ts, LANES), lambda bi, i: (i, 0)),
            pl.BlockSpec((ts, LANES), lambda bi, i: (i, 0)),
        ],
        out_specs=[
            pl.BlockSpec((1, ts, d), lambda bi, i: (bi, i, 0)),
            pl.BlockSpec((1, ts, d), lambda bi, i: (bi, i, 0)),
            pl.BlockSpec((1, bpt, d, MOBA_BLOCK), lambda bi, i: (bi, i, 0, 0)),
            pl.BlockSpec((1, bpt, 1, d), lambda bi, i: (bi, i, 0, 0)),
        ],
        out_shape=[
            jax.ShapeDtypeStruct((b, s, d), BF16),
            jax.ShapeDtypeStruct((b, s, d), BF16),
            jax.ShapeDtypeStruct((b, nb, d, MOBA_BLOCK), BF16),
            jax.ShapeDtypeStruct((b, nb, 1, d), F32),
        ],
        compiler_params=_params(2),
        name="qkv_rope",
    )(x, g, wq, wk, wvt, cos, sin)


def _moba_kernel(q_ref, k_ref, vt_ref, km_ref, o_ref, bias_ref):
    i = pl.program_id(2)
    blk = MOBA_BLOCK
    nb = km_ref.shape[1]
    neg_inf = -jnp.inf
    q = q_ref[0]
    km = km_ref[0, :, 0, :]
    lane = lax.broadcasted_iota(jnp.int32, (1, LANES), 1)
    jrow = lax.broadcasted_iota(jnp.int32, (nb, blk), 0)
    past = jrow < i

    qms = []
    for hh in range(HEADS_PER_PAIR):
        in_head = (lane >= hh * HEAD_DIM) & (lane < (hh + 1) * HEAD_DIM)
        qm = jnp.where(in_head, q, jnp.zeros_like(q))
        qms.append(qm)
        kmm = jnp.where(in_head, km, 0.0)
        gate = lax.dot_general(kmm, qm.astype(F32), _NT, precision=HIGHEST,
                               preferred_element_type=F32)
        g = jnp.where(past, gate, neg_inf)
        sel = jnp.zeros((nb, blk), jnp.bool_)
        for _ in range(MOBA_TOPK):
            m = jnp.max(g, axis=0, keepdims=True)
            idx = jnp.min(jnp.where(g == m, jrow, nb), axis=0, keepdims=True)
            pick = jrow == idx
            sel = sel | pick
            g = jnp.where(pick, neg_inf, g)
        bias_ref[hh] = jnp.where(sel & past, 0.0, neg_inf)

    ones_rows = jnp.ones((HALO_BF16, blk), BF16)

    def v_aug(j, hh):
        vt = vt_ref[0, j, hh * HEAD_DIM:(hh + 1) * HEAD_DIM, :]
        return jnp.concatenate([vt, ones_rows], axis=0)

    krow = lax.broadcasted_iota(jnp.int32, (blk, blk), 0)
    qcol = lax.broadcasted_iota(jnp.int32, (blk, blk), 1)
    causal = krow <= qcol
    k_own = k_ref[0, pl.ds(pl.multiple_of(i * blk, blk), blk), :]
    state = []
    for hh in range(HEADS_PER_PAIR):
        s = lax.dot_general(k_own, qms[hh], _NT, preferred_element_type=F32)
        s = jnp.where(causal, s, neg_inf)
        m = jnp.max(s, axis=0, keepdims=True)
        p = jnp.exp2((s - m).astype(BF16))
        acc = jnp.dot(v_aug(i, hh), p, preferred_element_type=F32)
        state += [m, acc]

    grp = MOBA_GROUP
    assert nb % grp == 0

    def body(g, carry):
        j0 = g * grp
        k_grp = k_ref[0, pl.ds(pl.multiple_of(j0 * blk, grp * blk), grp * blk), :]
        out = []
        scores = [lax.dot_general(k_grp, qm, _NT, preferred_element_type=F32) for qm in qms]
        for hh in range(HEADS_PER_PAIR):
            m, acc = carry[2 * hh:2 * hh + 2]
            s = scores[hh]
            s_j = [s[jj * blk:(jj + 1) * blk] + bias_ref[hh, pl.ds(j0 + jj, 1), :]
                   for jj in range(grp)]
            m_new = m
            for t in s_j:
                m_new = jnp.maximum(m_new, jnp.max(t, axis=0, keepdims=True))
            alpha = jnp.exp2(m - m_new)
            p = jnp.concatenate([jnp.exp2((t - m_new).astype(BF16)) for t in s_j], axis=0)
            vt = jnp.concatenate([v_aug(j0 + jj, hh) for jj in range(grp)], axis=1)
            acc = alpha * acc + jnp.dot(vt, p, preferred_element_type=F32)
            out += [m_new, acc]
        return tuple(out)

    n_groups = lax.shift_right_logical(i + (grp - 1), int(math.log2(grp)))
    state = lax.fori_loop(0, n_groups, body, tuple(state))
    outs = []
    for hh in range(HEADS_PER_PAIR):
        acc = state[2 * hh + 1]
        outs.append(acc[:HEAD_DIM] / acc[HEAD_DIM:HEAD_DIM + 1])
    o_ref[0] = jnp.concatenate(outs, axis=0).T.astype(o_ref.dtype)


def _moba_call(q, k, vt, km):
    b, s, d = q.shape
    nb = s // MOBA_BLOCK
    return pl.pallas_call(
        _moba_kernel,
        grid=(b, d // LANES, nb),
        in_specs=[
            pl.BlockSpec((1, MOBA_BLOCK, LANES), lambda bi, hp, i: (bi, i, hp)),
            pl.BlockSpec((1, s, LANES), lambda bi, hp, i: (bi, 0, hp)),
            pl.BlockSpec((1, nb, LANES, MOBA_BLOCK), lambda bi, hp, i: (bi, 0, hp, 0)),
            pl.BlockSpec((1, nb, 1, LANES), lambda bi, hp, i: (bi, 0, 0, hp)),
        ],
        out_specs=pl.BlockSpec((1, MOBA_BLOCK, LANES), lambda bi, hp, i: (bi, i, hp)),
        out_shape=jax.ShapeDtypeStruct((b, s, d), BF16),
        scratch_shapes=[pltpu.VMEM((HEADS_PER_PAIR, nb, MOBA_BLOCK), F32)],
        compiler_params=_params(3),
        name="moba_attention",
    )(q, k, vt, km)


def _oproj_kernel(*refs, gated):
    if gated:
        h_ref, y_ref, g_ref, w_ref, o_ref = refs
        y = y_ref[0].astype(F32) * g_ref[0]
    else:
        h_ref, y_ref, w_ref, o_ref = refs
        y = y_ref[0]
    o_ref[0] = h_ref[0] + jnp.dot(y.astype(BF16), w_ref[...], preferred_element_type=F32)


def _oproj_call(h, y, gate, w):
    b, s, d = h.shape
    ts = TOKEN_TILE
    tile = pl.BlockSpec((1, ts, d), lambda bi, i: (bi, i, 0))
    gated = gate is not None
    args = (h, y, gate, w) if gated else (h, y, w)
    return pl.pallas_call(
        functools.partial(_oproj_kernel, gated=gated),
        grid=(b, s // ts),
        in_specs=[tile] * (len(args) - 1) + [_const_spec((d, d))],
        out_specs=tile,
        out_shape=jax.ShapeDtypeStruct((b, s, d), F32),
        compiler_params=_params(2),
        name="out_proj_gated" if gated else "out_proj",
    )(*args)


def _ffn_kernel(*refs, final):
    if final:
        (x_ref, halo_ref, g_ref, wg_ref, wu_ref, cw_ref, cb_ref, wd_ref, gf_ref,
         o_ref, hn_ref, gx_ref, acc_ref) = refs
    else:
        (x_ref, halo_ref, g_ref, wg_ref, wu_ref, cw_ref, cb_ref, wd_ref,
         o_ref, hn_ref, gx_ref, acc_ref) = refs
    i = pl.program_id(1)
    ts = x_ref.shape[1]
    halo = HALO_BF16
    x = x_ref[0]
    g = g_ref[...]
    hn_ref[0:halo, :] = _rms(halo_ref[0], g).astype(BF16)
    hn_ref[halo:, :] = _rms(x, g).astype(BF16)
    acc_ref[...] = jnp.zeros_like(acc_ref)

    def chunk(c, _):
        gx_ref[...] = jnp.dot(hn_ref[...], wg_ref[c], preferred_element_type=F32)

        @pl.when(i == 0)
        def _():
            gx_ref[0:halo, :] = jnp.zeros((halo, gx_ref.shape[1]), F32)

        cw = cw_ref[c]
        conv = (gx_ref[halo - 2:halo - 2 + ts, :] * cw[0:1]
                + gx_ref[halo - 1:halo - 1 + ts, :] * cw[1:2]
                + gx_ref[halo:halo + ts, :] * cw[2:3]
                + cb_ref[c])
        up = jnp.dot(hn_ref[halo:, :], wu_ref[c], preferred_element_type=F32)
        act = (conv * jax.nn.sigmoid(conv) * up).astype(BF16)
        acc_ref[...] += jnp.dot(act, wd_ref[c], preferred_element_type=F32)
        return 0

    lax.fori_loop(0, wg_ref.shape[0], chunk, 0)
    out = x + acc_ref[...]
    if final:
        out = _rms(out, gf_ref[...])
    o_ref[0] = out


def _ffn_call(h, g, wg, wu, cw, cb, wd, g_final):
    b, s, d = h.shape
    ts = TOKEN_TILE
    nc, _, fc = wg.shape
    halo = HALO_BF16
    hpt = ts // halo
    final = g_final is not None
    tile = pl.BlockSpec((1, ts, d), lambda bi, i: (bi, i, 0))
    in_specs = [
        tile,
        pl.BlockSpec((1, halo, d), lambda bi, i: (bi, jnp.maximum(i * hpt - 1, 0), 0)),
        _const_spec((1, d)),
        _const_spec((nc, d, fc)),
        _const_spec((nc, d, fc)),
        _const_spec((nc, CONV_WIDTH, fc)),
        _const_spec((nc, 1, fc)),
        _const_spec((nc, fc, d)),
    ]
    args = [h, h, g, wg, wu, cw, cb, wd]
    if final:
        in_specs.append(_const_spec((1, d)))
        args.append(g_final)
    return pl.pallas_call(
        functools.partial(_ffn_kernel, final=final),
        grid=(b, s // ts),
        in_specs=in_specs,
        out_specs=tile,
        out_shape=jax.ShapeDtypeStruct((b, s, d), F32),
        scratch_shapes=[
            pltpu.VMEM((ts + halo, d), BF16),
            pltpu.VMEM((ts + halo, fc), F32),
            pltpu.VMEM((ts, d), F32),
        ],
        compiler_params=_params(2),
        name="conv_ffn_final" if final else "conv_ffn",
    )(*args)


def _rwkv_proj_kernel(x_ref, halo_ref, gm_ref, mu_ref, wr_ref, wk_ref, wv_ref,
                      w0_ref, w1_ref, w2_ref, a0_ref, a1_ref, a2_ref, g1_ref, g2_ref,
                      r_ref, k_ref, v_ref, lw_ref, a_ref, g_ref, xs_ref):
    i = pl.program_id(1)
    ts = x_ref.shape[1]
    halo = HALO_F32
    gm = gm_ref[...]
    xn = _rms(x_ref[0], gm)
    xs_ref[0:halo, :] = _rms(halo_ref[0], gm)
    xs_ref[halo:, :] = xn

    @pl.when(i == 0)
    def _():
        xs_ref[0:halo, :] = jnp.zeros((halo, xs_ref.shape[1]), F32)

    xx = xs_ref[halo - 1:halo - 1 + ts, :] - xn

    def mix(n):
        return (xn + xx * mu_ref[n:n + 1, :]).astype(BF16)

    r_ref[0] = jnp.dot(mix(0), wr_ref[...], preferred_element_type=F32)
    k_ref[0] = jnp.dot(mix(2), wk_ref[...], preferred_element_type=F32)
    v_ref[0] = jnp.dot(mix(3), wv_ref[...], preferred_element_type=F32)

    wl = w0_ref[...] + _mm(jnp.tanh(jnp.dot(mix(1), w1_ref[...], preferred_element_type=F32)),
                           w2_ref[...])
    z = -wl
    softplus = jnp.maximum(z, 0.0) + jnp.log(1.0 + jnp.exp(-jnp.abs(z)))
    w = -softplus - 0.5
    lw_ref[0] = -jnp.exp(w)

    al = a0_ref[...] + _mm(jnp.dot(mix(4), a1_ref[...], preferred_element_type=F32), a2_ref[...])
    a_ref[0] = jax.nn.sigmoid(al)
    g_ref[0] = _mm(jax.nn.sigmoid(jnp.dot(mix(5), g1_ref[...], preferred_element_type=F32)),
                   g2_ref[...])


def _rwkv_proj_call(h, gm, mu, wr, wk, wv, w0, w1, w2, a0, a1, a2, g1, g2):
    b, s, d = h.shape
    ts = RWKV_STEP
    halo = HALO_F32
    hpt = ts // halo
    tile = pl.BlockSpec((1, ts, d), lambda bi, i: (bi, i, 0))
    consts = [gm, mu, wr, wk, wv, w0, w1, w2, a0, a1, a2, g1, g2]
    return pl.pallas_call(
        _rwkv_proj_kernel,
        grid=(b, s // ts),
        in_specs=[tile,
                  pl.BlockSpec((1, halo, d), lambda bi, i: (bi, jnp.maximum(i * hpt - 1, 0), 0))]
                 + [_const_spec(c.shape) for c in consts],
        out_specs=[tile] * 6,
        out_shape=[jax.ShapeDtypeStruct((b, s, d), F32)] * 6,
        scratch_shapes=[pltpu.VMEM((ts + halo, d), F32)],
        compiler_params=_params(2),
        name="rwkv_proj",
    )(h, h, *consts)


def _rwkv_core_kernel(r_ref, k_ref, v_ref, lw_ref, a_ref, kk_ref, ka_ref, rk_ref,
                      lnw_ref, lnb_ref, o_ref, h_ref):
    c = RWKV_CHUNK
    hd = HEAD_DIM
    c_shift = int(math.log2(c))
    step = r_ref.shape[1]
    nch = step // c

    @pl.when(pl.program_id(2) == 0)
    def _():
        h_ref[...] = jnp.zeros_like(h_ref)

    lane = lax.broadcasted_iota(jnp.int32, (1, LANES), 1)
    head0 = lane < hd

    def by_head(x0, x1):
        return jnp.where(head0, x0, x1)

    def seg_sum(t):
        s0 = jnp.sum(jnp.where(head0, t, 0.0), axis=-1, keepdims=True)
        s1 = jnp.sum(jnp.where(head0, 0.0, t), axis=-1, keepdims=True)
        return by_head(s0, s1)

    r, k, v, lw, a = r_ref[0], k_ref[0], v_ref[0], lw_ref[0], a_ref[0]

    tr = lax.broadcasted_iota(jnp.int32, (step, step), 0)
    tc = lax.broadcasted_iota(jnp.int32, (step, step), 1)
    same_chunk = lax.shift_right_logical(tr, c_shift) == lax.shift_right_logical(tc, c_shift)
    ltri = jnp.where((tr >= tc) & same_chunk, 1.0, 0.0).astype(BF16)
    hi = lw.astype(BF16)
    rem = lw - hi.astype(F32)
    mid = rem.astype(BF16)
    lo = (rem - mid.astype(F32)).astype(BF16)
    cum3 = jnp.dot(ltri, jnp.concatenate([hi, mid, lo], axis=1), preferred_element_type=F32)
    cum = cum3[:, :LANES] + cum3[:, LANES:2 * LANES] + cum3[:, 2 * LANES:]

    e_w = jnp.exp(cum)
    e_wi = jnp.exp(-cum)
    e_wp = jnp.exp(cum - lw)
    kk = k * kk_ref[...]
    kk = kk / jnp.maximum(jnp.sqrt(seg_sum(kk * kk)), 1e-12)
    kmod = k * (1.0 + (a - 1.0) * ka_ref[...])
    a_t = -kk * e_wp
    b_t = kk * a * e_wi
    k_t = kmod * e_wi
    r_t = r * e_w

    rows = [slice(ch * c, (ch + 1) * c) for ch in range(nch)]
    w_c = [e_w[ch * c + c - 1:ch * c + c, :] for ch in range(nch)]
    zeros_c = jnp.zeros((c, LANES), F32)

    r4 = lax.broadcasted_iota(jnp.int32, (4 * c, LANES), 0)
    l4 = lax.broadcasted_iota(jnp.int32, (4 * c, LANES), 1)
    t_idx = r4 & (c - 1)
    s_idx = l4 & (c - 1)
    incl = lax.shift_right_logical(r4, c_shift) & 1
    tri_mask = t_idx - s_idx + incl > 0
    r2 = lax.broadcasted_iota(jnp.int32, (LANES, LANES), 0)
    l2 = lax.broadcasted_iota(jnp.int32, (LANES, LANES), 1)
    eye = r2 == l2
    blockdiag = (r2 < hd) == (l2 < hd)
    eye_f = jnp.where(eye, 1.0, 0.0)

    aa, rr = [], []
    for rw in rows:
        a_c, r_c = a_t[rw], r_t[rw]
        lhs = jnp.concatenate([by_head(a_c, 0.0), by_head(r_c, 0.0),
                               by_head(0.0, a_c), by_head(0.0, r_c)], axis=0)
        rhs = jnp.concatenate([b_t[rw], k_t[rw]], axis=0)
        res = jnp.where(tri_mask, _mm_nt(lhs, rhs), 0.0)
        aa.append(jnp.concatenate([res[0:c], res[2 * c:3 * c]], axis=0))
        rr.append(jnp.concatenate([res[c:2 * c], res[3 * c:4 * c]], axis=0))

    n_bd = [jnp.where(blockdiag, by_head(x, pltpu.roll(x, hd, 1)), 0.0) for x in aa]
    inv = [eye_f + n for n in n_bd]
    p = n_bd
    for _ in range(c_shift - 1):
        p = [_mm(x, x) for x in p]
        inv = [t + _mm(t, x) for t, x in zip(inv, p)]

    akv = [_mm(x, jnp.concatenate([zeros_c, v[rw]], axis=0)) for x, rw in zip(aa, rows)]
    ua = [_mm(t, jnp.concatenate([x, jnp.concatenate([a_t[rw], a_t[rw]], axis=0)], axis=1))
          for t, x, rw in zip(inv, akv, rows)]
    u0 = [by_head(x[0:c, :LANES], x[c:2 * c, :LANES]) for x in ua]
    a2 = [by_head(x[0:c, LANES:], x[c:2 * c, LANES:]) for x in ua]

    rhs4 = [jnp.concatenate([jnp.concatenate([x, zeros_c], axis=0),
                             jnp.concatenate([u, v[rw]], axis=0)], axis=1)
            for x, u, rw in zip(a2, u0, rows)]
    res4 = [_mm(x, y) for x, y in zip(rr, rhs4)]
    rp = [r_t[rw] + by_head(x[0:c, :LANES], x[c:2 * c, :LANES]) for x, rw in zip(res4, rows)]
    y0 = [by_head(x[0:c, LANES:], x[c:2 * c, LANES:]) for x in res4]
    gd = [_mm_tn(jnp.concatenate([b_t[rw] * w, k_t[rw] * w], axis=0), y)
          for rw, w, y in zip(rows, w_c, rhs4)]
    gg = [jnp.where(blockdiag, x[:, :LANES], 0.0) + jnp.where(eye, w, 0.0) for x, w in zip(gd, w_c)]
    dd = [jnp.where(blockdiag, x[:, LANES:], 0.0) for x in gd]

    hh = h_ref[...]
    ys = []
    for ch in range(nch):
        ys.append(_mm(rp[ch], hh) + y0[ch])
        hh = _mm(gg[ch], hh) + dd[ch]
    h_ref[...] = hh
    y = jnp.concatenate(ys, axis=0)

    mean = seg_sum(y) * (1.0 / hd)
    dev = y - mean
    var = seg_sum(dev * dev) * (1.0 / hd)
    yn = dev * lax.rsqrt(var + LNX_EPS) * lnw_ref[...] + lnb_ref[...]
    bonus = seg_sum(r * kmod * rk_ref[...]) * v
    o_ref[0] = yn + bonus


def _rwkv_core_call(r, k, v, lw, a, k_k, k_a, r_k, ln_w, ln_b):
    b, s, d = r.shape
    step = RWKV_STEP
    tile = pl.BlockSpec((1, step, LANES), lambda bi, hp, i: (bi, i, hp))
    par = pl.BlockSpec((1, LANES), lambda bi, hp, i: (0, hp))
    return pl.pallas_call(
        _rwkv_core_kernel,
        grid=(b, d // LANES, s // step),
        in_specs=[tile] * 5 + [par] * 5,
        out_specs=tile,
        out_shape=jax.ShapeDtypeStruct((b, s, d), F32),
        scratch_shapes=[pltpu.VMEM((LANES, LANES), F32)],
        compiler_params=_params(3),
        name="rwkv_recurrence",
    )(r, k, v, lw, a, k_k, k_a, r_k, ln_w, ln_b)


def _rope_tables(seq):
    inv = 1.0 / (ROPE_THETA ** (jnp.arange(0, HEAD_DIM, 2, dtype=F32) / HEAD_DIM))
    ang = jnp.arange(seq, dtype=F32)[:, None] * inv[None, :]
    cos, sin = jnp.cos(ang), jnp.sin(ang)
    cos = jnp.concatenate([cos, cos] * HEADS_PER_PAIR, axis=-1)
    sin = jnp.concatenate([-sin, sin] * HEADS_PER_PAIR, axis=-1)
    return cos, sin


def _ffn_weights(w_gate, w_up, conv_w, conv_b, w_down):
    d, f = w_gate.shape
    fc = FFN_FCHUNK
    nc = f // fc
    wg = w_gate.astype(BF16).reshape(d, nc, fc).transpose(1, 0, 2)
    wu = w_up.astype(BF16).reshape(d, nc, fc).transpose(1, 0, 2)
    cw = conv_w.reshape(CONV_WIDTH, nc, fc).transpose(1, 0, 2)
    cb = conv_b.reshape(nc, 1, fc)
    wd = w_down.astype(BF16).reshape(nc, fc, d)
    return wg, wu, cw, cb, wd


def kernel(x, norm_mix, norm_ffn, norm_final, attn_w_qkv, attn_w_o, rwkv_mu, rwkv_w_rkv, rwkv_w0, rwkv_w1, rwkv_w2, rwkv_a0, rwkv_a1, rwkv_a2, rwkv_g1, rwkv_g2, rwkv_k_k, rwkv_k_a, rwkv_r_k, rwkv_lnx_w, rwkv_lnx_b, rwkv_w_o, ffn_w_gate, ffn_w_up, ffn_conv_w, ffn_conv_b, ffn_w_down):
    b, s, d = x.shape
    depth = norm_mix.shape[0]
    n_mixers = 2
    row = lambda t: t.reshape(1, d)
    cos, sin = _rope_tables(s)
    h = x
    for i in range(depth):
        j = i // n_mixers
        if i % n_mixers == 0:
            w_qkv = attn_w_qkv[j].astype(BF16)
            q, k, vt, km = _qkv_call(h, row(norm_mix[i]), w_qkv[:, :d], w_qkv[:, d:2 * d],
                                     w_qkv[:, 2 * d:].T, cos, sin)
            attn = _moba_call(q, k, vt, km)
            h = _oproj_call(h, attn, None, attn_w_o[j].astype(BF16))
        else:
            bf = lambda t: t.astype(BF16)
            r, k, v, lw, a, g = _rwkv_proj_call(
                h, row(norm_mix[i]), rwkv_mu[j], bf(rwkv_w_rkv[j, 0]), bf(rwkv_w_rkv[j, 1]),
                bf(rwkv_w_rkv[j, 2]), row(rwkv_w0[j]), bf(rwkv_w1[j]), bf(rwkv_w2[j]),
                row(rwkv_a0[j]), bf(rwkv_a1[j]), bf(rwkv_a2[j]), bf(rwkv_g1[j]), bf(rwkv_g2[j]))
            y = _rwkv_core_call(r, k, v, lw, a, row(rwkv_k_k[j]), row(rwkv_k_a[j]),
                                row(rwkv_r_k[j]), row(rwkv_lnx_w[j]), row(rwkv_lnx_b[j]))
            h = _oproj_call(h, y, g, bf(rwkv_w_o[j]))
        g_final = row(norm_final) if i == depth - 1 else None
        h = _ffn_call(h, row(norm_ffn[i]),
                      *_ffn_weights(ffn_w_gate[i], ffn_w_up[i], ffn_conv_w[i], ffn_conv_b[i],
                                    ffn_w_down[i]), g_final)
    return h
```

```python
import functools
import math

import jax
import jax.numpy as jnp
from jax import lax
from jax.experimental import pallas as pl
from jax.experimental.pallas import tpu as pltpu

F32 = jnp.float32
BF16 = jnp.bfloat16
HIGHEST = lax.Precision.HIGHEST

LANES = 128
HEAD_DIM = 64
HEADS_PER_PAIR = LANES // HEAD_DIM
MOBA_BLOCK = 256
MOBA_TOPK = 3
MOBA_GROUP = 4
MOBA_HEADS = 4
ROPE_THETA = 10000.0
RMS_EPS = 1e-6
LNX_EPS = 64e-5
CONV_WIDTH = 3
RWKV_CHUNK = 64
RWKV_STEP = 1024
RWKV_CUMSUM_ROWS = 256
RWKV_PROJ_TILE = 256
FFN_FCHUNK = 256
TOKEN_TILE = 512
HALO_F32 = 8
HALO_BF16 = 16
VMEM_LIMIT = 56 * 1024 * 1024

_NT = (((1,), (1,)), ((), ()))
_TN = (((0,), (0,)), ((), ()))


def _mm(a, b):
    return jnp.dot(a.astype(BF16), b.astype(BF16), preferred_element_type=F32)


def _mm_nt(a, b):
    return lax.dot_general(a.astype(BF16), b.astype(BF16), _NT, preferred_element_type=F32)


def _mm_tn(a, b):
    return lax.dot_general(a.astype(BF16), b.astype(BF16), _TN, preferred_element_type=F32)


def _rms(x, g):
    y = x * lax.rsqrt(jnp.mean(x * x, axis=-1, keepdims=True) + RMS_EPS)
    return y * g


def _params(n_axes):
    return pltpu.CompilerParams(
        dimension_semantics=("arbitrary",) * n_axes, vmem_limit_bytes=VMEM_LIMIT)


def _const_spec(shape):
    zeros = (0,) * len(shape)
    return pl.BlockSpec(shape, lambda *_: zeros)


def _qkv_kernel(x_ref, g_ref, wq_ref, wk_ref, wvt_ref, cos_ref, sin_ref,
                q_ref, k_ref, vt_ref, km_ref, *, scale):
    ts, d = x_ref.shape[1], x_ref.shape[2]
    xb = _rms(x_ref[0], g_ref[...]).astype(BF16)
    q = jnp.dot(xb, wq_ref[...], preferred_element_type=F32)
    k = jnp.dot(xb, wk_ref[...], preferred_element_type=F32)
    vt = lax.dot_general(wvt_ref[...], xb, _NT, preferred_element_type=F32)

    reps = d // LANES
    cos = jnp.concatenate([cos_ref[...]] * reps, axis=1)
    sin = jnp.concatenate([sin_ref[...]] * reps, axis=1)
    lane = lax.broadcasted_iota(jnp.int32, (1, d), 1)
    first_half = (lane & (HEAD_DIM - 1)) < HEAD_DIM // 2

    def rope(t):
        partner = jnp.where(first_half,
                            pltpu.roll(t, d - HEAD_DIM // 2, 1),
                            pltpu.roll(t, HEAD_DIM // 2, 1))
        return t * cos + partner * sin

    q = rope(q) * scale
    k = rope(k)
    q_ref[0] = q.astype(BF16)
    k_ref[0] = k.astype(BF16)
    for r in range(ts // MOBA_BLOCK):
        rows = slice(r * MOBA_BLOCK, (r + 1) * MOBA_BLOCK)
        vt_ref[0, r] = vt[:, rows].astype(BF16)
        km_ref[0, r] = jnp.mean(k[rows], axis=0, keepdims=True)


def _qkv_call(x, g, wq, wk, wvt, cos, sin):
    b, s, d = x.shape
    ts = TOKEN_TILE
    nb = s // MOBA_BLOCK
    bpt = ts // MOBA_BLOCK
    scale = math.log2(math.e) / math.sqrt(HEAD_DIM)
    return pl.pallas_call(
        functools.partial(_qkv_kernel, scale=scale),
        grid=(b, s // ts),
        in_specs=[
            pl.BlockSpec((1, ts, d), lambda bi, i: (bi, i, 0)),
            _const_spec((1, d)),
            _const_spec((d, d)),
            _const_spec((d, d)),
            _const_spec((d, d)),
            pl.BlockSpec((ts, LANES), lambda bi, i: (i, 0)),
            pl.BlockSpec((ts, LANES), lambda bi, i: (i, 0)),
        ],
        out_specs=[
            pl.BlockSpec((1, ts, d), lambda bi, i: (bi, i, 0)),
            pl.BlockSpec((1, ts, d), lambda bi, i: (bi, i, 0)),
            pl.BlockSpec((1, bpt, d, MOBA_BLOCK), lambda bi, i: (bi, i, 0, 0)),
            pl.BlockSpec((1, bpt, 1, d), lambda bi, i: (bi, i, 0, 0)),
        ],
        out_shape=[
            jax.ShapeDtypeStruct((b, s, d), BF16),
            jax.ShapeDtypeStruct((b, s, d), BF16),
            jax.ShapeDtypeStruct((b, nb, d, MOBA_BLOCK), BF16),
            jax.ShapeDtypeStruct((b, nb, 1, d), F32),
        ],
        compiler_params=_params(2),
        name="qkv_rope",
    )(x, g, wq, wk, wvt, cos, sin)


def _moba_kernel(q_ref, k_ref, vt_ref, km_ref, o_ref, bias_ref):
    i = pl.program_id(2)
    blk = MOBA_BLOCK
    nb = km_ref.shape[1]
    neg_inf = -jnp.inf
    width = q_ref.shape[2]
    heads = range(width // HEAD_DIM)
    q = q_ref[0]
    km = km_ref[0, :, 0, :]
    lane = lax.broadcasted_iota(jnp.int32, (1, width), 1)
    jrow = lax.broadcasted_iota(jnp.int32, (nb, blk), 0)
    past = jrow < i

    qms = []
    for hh in heads:
        in_head = (lane >= hh * HEAD_DIM) & (lane < (hh + 1) * HEAD_DIM)
        qm = jnp.where(in_head, q, jnp.zeros_like(q))
        qms.append(qm)
        kmm = jnp.where(in_head, km, 0.0)
        gate = lax.dot_general(kmm, qm.astype(F32), _NT, precision=HIGHEST,
                               preferred_element_type=F32)
        g = jnp.where(past, gate, neg_inf)
        sel = jnp.zeros((nb, blk), jnp.bool_)
        for _ in range(MOBA_TOPK):
            m = jnp.max(g, axis=0, keepdims=True)
            idx = jnp.min(jnp.where(g == m, jrow, nb), axis=0, keepdims=True)
            pick = jrow == idx
            sel = sel | pick
            g = jnp.where(pick, neg_inf, g)
        bias_ref[hh] = jnp.where(sel & past, 0.0, neg_inf)

    ones_rows = jnp.ones((HALO_BF16, blk), BF16)

    def v_aug(j, hh):
        vt = vt_ref[0, j, hh * HEAD_DIM:(hh + 1) * HEAD_DIM, :]
        return jnp.concatenate([vt, ones_rows], axis=0)

    krow = lax.broadcasted_iota(jnp.int32, (blk, blk), 0)
    qcol = lax.broadcasted_iota(jnp.int32, (blk, blk), 1)
    causal = krow <= qcol
    k_own = k_ref[0, pl.ds(pl.multiple_of(i * blk, blk), blk), :]
    state = []
    own_scores = [lax.dot_general(k_own, qm, _NT, preferred_element_type=F32) for qm in qms]
    for hh in heads:
        s = jnp.where(causal, own_scores[hh], neg_inf)
        m = jnp.max(s, axis=0, keepdims=True).astype(BF16).astype(F32)
        p = jnp.exp2((s - m).astype(BF16))
        acc = jnp.dot(v_aug(i, hh), p, preferred_element_type=F32)
        state += [m, acc]

    grp = MOBA_GROUP
    assert nb % grp == 0

    def body(g, carry):
        j0 = g * grp
        k_grp = k_ref[0, pl.ds(pl.multiple_of(j0 * blk, grp * blk), grp * blk), :]
        out = []
        scores = [lax.dot_general(k_grp, qm, _NT, preferred_element_type=F32).astype(BF16)
                  for qm in qms]
        for hh in heads:
            m, acc = carry[2 * hh:2 * hh + 2]
            s = scores[hh]
            s_j = [s[jj * blk:(jj + 1) * blk] + bias_ref[hh, pl.ds(j0 + jj, 1), :].astype(BF16)
                   for jj in range(grp)]
            m_blk = jnp.max(s_j[0], axis=0, keepdims=True)
            for t in s_j[1:]:
                m_blk = jnp.maximum(m_blk, jnp.max(t, axis=0, keepdims=True))
            m_new = jnp.maximum(m, m_blk.astype(F32))
            alpha = jnp.exp2(m - m_new)
            m_b = m_new.astype(BF16)
            p = jnp.concatenate([jnp.exp2(t - m_b) for t in s_j], axis=0)
            vt = jnp.concatenate([v_aug(j0 + jj, hh) for jj in range(grp)], axis=1)
            acc = alpha * acc + jnp.dot(vt, p, preferred_element_type=F32)
            out += [m_new, acc]
        return tuple(out)

    n_groups = lax.shift_right_logical(i + (grp - 1), int(math.log2(grp)))
    state = lax.fori_loop(0, n_groups, body, tuple(state))
    outs = []
    for hh in heads:
        acc = state[2 * hh + 1]
        outs.append(acc[:HEAD_DIM] / acc[HEAD_DIM:HEAD_DIM + 1])
    o_ref[0] = jnp.concatenate(outs, axis=0).T.astype(o_ref.dtype)


def _moba_call(q, k, vt, km):
    b, s, d = q.shape
    nb = s // MOBA_BLOCK
    width = MOBA_HEADS * HEAD_DIM
    return pl.pallas_call(
        _moba_kernel,
        grid=(b, d // width, nb),
        in_specs=[
            pl.BlockSpec((1, MOBA_BLOCK, width), lambda bi, hg, i: (bi, i, hg)),
            pl.BlockSpec((1, s, width), lambda bi, hg, i: (bi, 0, hg)),
            pl.BlockSpec((1, nb, width, MOBA_BLOCK), lambda bi, hg, i: (bi, 0, hg, 0)),
            pl.BlockSpec((1, nb, 1, width), lambda bi, hg, i: (bi, 0, 0, hg)),
        ],
        out_specs=pl.BlockSpec((1, MOBA_BLOCK, width), lambda bi, hg, i: (bi, i, hg)),
        out_shape=jax.ShapeDtypeStruct((b, s, d), BF16),
        scratch_shapes=[pltpu.VMEM((MOBA_HEADS, nb, MOBA_BLOCK), F32)],
        compiler_params=_params(3),
        name="moba_attention",
    )(q, k, vt, km)


def _oproj_kernel(*refs, gated):
    if gated:
        h_ref, y_ref, g_ref, w_ref, o_ref = refs
        y = y_ref[0].astype(F32) * g_ref[0]
    else:
        h_ref, y_ref, w_ref, o_ref = refs
        y = y_ref[0]
    o_ref[0] = h_ref[0] + jnp.dot(y.astype(BF16), w_ref[...], preferred_element_type=F32)


def _oproj_call(h, y, gate, w):
    b, s, d = h.shape
    ts = TOKEN_TILE
    tile = pl.BlockSpec((1, ts, d), lambda bi, i: (bi, i, 0))
    gated = gate is not None
    args = (h, y, gate, w) if gated else (h, y, w)
    return pl.pallas_call(
        functools.partial(_oproj_kernel, gated=gated),
        grid=(b, s // ts),
        in_specs=[tile] * (len(args) - 1) + [_const_spec((d, d))],
        out_specs=tile,
        out_shape=jax.ShapeDtypeStruct((b, s, d), F32),
        compiler_params=_params(2),
        name="out_proj_gated" if gated else "out_proj",
    )(*args)


def _ffn_kernel(*refs, final):
    if final:
        (x_ref, halo_ref, g_ref, wg_ref, wu_ref, cw_ref, cb_ref, wd_ref, gf_ref,
         o_ref, hn_ref, gx_ref, act_ref) = refs
    else:
        (x_ref, halo_ref, g_ref, wg_ref, wu_ref, cw_ref, cb_ref, wd_ref,
         o_ref, hn_ref, gx_ref, act_ref) = refs
    i = pl.program_id(1)
    ts = x_ref.shape[1]
    nc, _, fc = wg_ref.shape
    halo = HALO_BF16
    x = x_ref[0]
    g = g_ref[...]
    hn_ref[0:halo, :] = _rms(halo_ref[0], g).astype(BF16)
    hn_ref[halo:, :] = _rms(x, g).astype(BF16)
    halo_keep = jnp.where(i == 0, 0.0, 1.0)

    for c in range(nc):
        gx = gx_ref.at[c % 2]
        gate = jnp.dot(hn_ref[...], wg_ref[c], preferred_element_type=F32)
        gx[0:halo, :] = gate[0:halo] * halo_keep
        gx[halo:, :] = gate[halo:]
        up = jnp.dot(hn_ref[halo:, :], wu_ref[c], preferred_element_type=F32)
        cw = cw_ref[c]
        conv = (gx[halo - 2:halo - 2 + ts, :] * cw[0:1]
                + gx[halo - 1:halo - 1 + ts, :] * cw[1:2]
                + gate[halo:] * cw[2:3]
                + cb_ref[c])
        act_ref[:, c * fc:(c + 1) * fc] = (conv * jax.nn.sigmoid(conv) * up).astype(BF16)

    out = x + jnp.dot(act_ref[...], wd_ref[...], preferred_element_type=F32)
    if final:
        out = _rms(out, gf_ref[...])
    o_ref[0] = out


def _ffn_call(h, g, wg, wu, cw, cb, wd, g_final):
    b, s, d = h.shape
    ts = TOKEN_TILE
    nc, _, fc = wg.shape
    halo = HALO_BF16
    hpt = ts // halo
    final = g_final is not None
    tile = pl.BlockSpec((1, ts, d), lambda bi, i: (bi, i, 0))
    in_specs = [
        tile,
        pl.BlockSpec((1, halo, d), lambda bi, i: (bi, jnp.maximum(i * hpt - 1, 0), 0)),
        _const_spec((1, d)),
        _const_spec((nc, d, fc)),
        _const_spec((nc, d, fc)),
        _const_spec((nc, CONV_WIDTH, fc)),
        _const_spec((nc, 1, fc)),
        _const_spec((nc * fc, d)),
    ]
    args = [h, h, g, wg, wu, cw, cb, wd]
    if final:
        in_specs.append(_const_spec((1, d)))
        args.append(g_final)
    return pl.pallas_call(
        functools.partial(_ffn_kernel, final=final),
        grid=(b, s // ts),
        in_specs=in_specs,
        out_specs=tile,
        out_shape=jax.ShapeDtypeStruct((b, s, d), F32),
        scratch_shapes=[
            pltpu.VMEM((ts + halo, d), BF16),
            pltpu.VMEM((2, ts + halo, fc), F32),
            pltpu.VMEM((ts, nc * fc), BF16),
        ],
        compiler_params=_params(2),
        name="conv_ffn_final" if final else "conv_ffn",
    )(*args)


def _rwkv_proj_kernel(x_ref, halo_ref, gm_ref, mu_ref, wr_ref, wk_ref, wv_ref,
                      w0_ref, w1_ref, w2_ref, a0_ref, a1_ref, a2_ref, g1_ref, g2_ref,
                      r_ref, k_ref, v_ref, lw_ref, a_ref, g_ref, xs_ref):
    i = pl.program_id(1)
    ts = x_ref.shape[1]
    halo = HALO_F32
    gm = gm_ref[...]
    xn = _rms(x_ref[0], gm)
    xs_ref[0:halo, :] = _rms(halo_ref[0], gm)
    xs_ref[halo:, :] = xn

    @pl.when(i == 0)
    def _():
        xs_ref[0:halo, :] = jnp.zeros((halo, xs_ref.shape[1]), F32)

    xx = xs_ref[halo - 1:halo - 1 + ts, :] - xn

    def mix(n):
        return (xn + xx * mu_ref[n:n + 1, :]).astype(BF16)

    r_ref[0] = jnp.dot(mix(0), wr_ref[...], preferred_element_type=F32)
    k_ref[0] = jnp.dot(mix(2), wk_ref[...], preferred_element_type=F32)
    v_ref[0] = jnp.dot(mix(3), wv_ref[...], preferred_element_type=F32)

    wl = w0_ref[...] + _mm(jnp.tanh(jnp.dot(mix(1), w1_ref[...], preferred_element_type=F32)),
                           w2_ref[...])
    z = -wl
    softplus = jnp.maximum(z, 0.0) + jnp.log(1.0 + jnp.exp(-jnp.abs(z)))
    w = -softplus - 0.5
    lw_ref[0] = -jnp.exp(w)

    al = a0_ref[...] + _mm(jnp.dot(mix(4), a1_ref[...], preferred_element_type=F32), a2_ref[...])
    a_ref[0] = jax.nn.sigmoid(al)
    g_ref[0] = _mm(jax.nn.sigmoid(jnp.dot(mix(5), g1_ref[...], preferred_element_type=F32)),
                   g2_ref[...])


def _rwkv_proj_call(h, gm, mu, wr, wk, wv, w0, w1, w2, a0, a1, a2, g1, g2):
    b, s, d = h.shape
    ts = RWKV_PROJ_TILE
    halo = HALO_F32
    hpt = ts // halo
    tile = pl.BlockSpec((1, ts, d), lambda bi, i: (bi, i, 0))
    consts = [gm, mu, wr, wk, wv, w0, w1, w2, a0, a1, a2, g1, g2]
    return pl.pallas_call(
        _rwkv_proj_kernel,
        grid=(b, s // ts),
        in_specs=[tile,
                  pl.BlockSpec((1, halo, d), lambda bi, i: (bi, jnp.maximum(i * hpt - 1, 0), 0))]
                 + [_const_spec(c.shape) for c in consts],
        out_specs=[tile] * 6,
        out_shape=[jax.ShapeDtypeStruct((b, s, d), F32)] * 6,
        scratch_shapes=[pltpu.VMEM((ts + halo, d), F32)],
        compiler_params=_params(2),
        name="rwkv_proj",
    )(h, h, *consts)


def _rwkv_core_kernel(r_ref, k_ref, v_ref, lw_ref, a_ref, kk_ref, ka_ref, rk_ref,
                      lnw_ref, lnb_ref, o_ref, h_ref):
    c = RWKV_CHUNK
    hd = HEAD_DIM
    c_shift = int(math.log2(c))
    step = r_ref.shape[1]
    nch = step // c

    @pl.when(pl.program_id(2) == 0)
    def _():
        h_ref[...] = jnp.zeros_like(h_ref)

    lane = lax.broadcasted_iota(jnp.int32, (1, LANES), 1)
    head0 = lane < hd

    def by_head(x0, x1):
        return jnp.where(head0, x0, x1)

    def seg_sum(t):
        s0 = jnp.sum(jnp.where(head0, t, 0.0), axis=-1, keepdims=True)
        s1 = jnp.sum(jnp.where(head0, 0.0, t), axis=-1, keepdims=True)
        return by_head(s0, s1)

    r, k, v, lw, a = r_ref[0], k_ref[0], v_ref[0], lw_ref[0], a_ref[0]

    sub = min(step, RWKV_CUMSUM_ROWS)
    tr = lax.broadcasted_iota(jnp.int32, (sub, sub), 0)
    tc = lax.broadcasted_iota(jnp.int32, (sub, sub), 1)
    same_chunk = lax.shift_right_logical(tr, c_shift) == lax.shift_right_logical(tc, c_shift)
    ltri = jnp.where((tr >= tc) & same_chunk, 1.0, 0.0).astype(BF16)
    hi = lw.astype(BF16)
    rem = lw - hi.astype(F32)
    mid = rem.astype(BF16)
    lo = (rem - mid.astype(F32)).astype(BF16)
    split = jnp.concatenate([hi, mid, lo], axis=1)
    cum3 = jnp.concatenate(
        [jnp.dot(ltri, split[n * sub:(n + 1) * sub], preferred_element_type=F32)
         for n in range(step // sub)], axis=0)
    cum = cum3[:, :LANES] + cum3[:, LANES:2 * LANES] + cum3[:, 2 * LANES:]

    e_w = jnp.exp(cum)
    e_wi = jnp.exp(-cum)
    e_wp = jnp.exp(cum - lw)
    kk = k * kk_ref[...]
    kk = kk / jnp.maximum(jnp.sqrt(seg_sum(kk * kk)), 1e-12)
    kmod = k * (1.0 + (a - 1.0) * ka_ref[...])
    a_t = -kk * e_wp
    b_t = kk * a * e_wi
    k_t = kmod * e_wi
    r_t = r * e_w

    rows = [slice(ch * c, (ch + 1) * c) for ch in range(nch)]
    w_c = [e_w[ch * c + c - 1:ch * c + c, :] for ch in range(nch)]
    zeros_c = jnp.zeros((c, LANES), F32)

    r4 = lax.broadcasted_iota(jnp.int32, (4 * c, LANES), 0)
    l4 = lax.broadcasted_iota(jnp.int32, (4 * c, LANES), 1)
    t_idx = r4 & (c - 1)
    s_idx = l4 & (c - 1)
    incl = lax.shift_right_logical(r4, c_shift) & 1
    tri_mask = t_idx - s_idx + incl > 0
    r2 = lax.broadcasted_iota(jnp.int32, (LANES, LANES), 0)
    l2 = lax.broadcasted_iota(jnp.int32, (LANES, LANES), 1)
    eye = r2 == l2
    blockdiag = (r2 < hd) == (l2 < hd)
    eye_f = jnp.where(eye, 1.0, 0.0)

    aa, rr = [], []
    for rw in rows:
        a_c, r_c = a_t[rw], r_t[rw]
        lhs = jnp.concatenate([by_head(a_c, 0.0), by_head(r_c, 0.0),
                               by_head(0.0, a_c), by_head(0.0, r_c)], axis=0)
        rhs = jnp.concatenate([b_t[rw], k_t[rw]], axis=0)
        res = jnp.where(tri_mask, _mm_nt(lhs, rhs), 0.0)
        aa.append(jnp.concatenate([res[0:c], res[2 * c:3 * c]], axis=0))
        rr.append(jnp.concatenate([res[c:2 * c], res[3 * c:4 * c]], axis=0))

    n_bd = [jnp.where(blockdiag, by_head(x, pltpu.roll(x, hd, 1)), 0.0) for x in aa]
    inv = [eye_f + n for n in n_bd]
    p = n_bd
    for _ in range(c_shift - 1):
        p = [_mm(x, x) for x in p]
        inv = [t + _mm(t, x) for t, x in zip(inv, p)]

    akv = [_mm(x, jnp.concatenate([zeros_c, v[rw]], axis=0)) for x, rw in zip(aa, rows)]
    ua = [_mm(t, jnp.concatenate([x, jnp.concatenate([a_t[rw], a_t[rw]], axis=0)], axis=1))
          for t, x, rw in zip(inv, akv, rows)]
    u0 = [by_head(x[0:c, :LANES], x[c:2 * c, :LANES]) for x in ua]
    a2 = [by_head(x[0:c, LANES:], x[c:2 * c, LANES:]) for x in ua]

    rhs4 = [jnp.concatenate([jnp.concatenate([x, zeros_c], axis=0),
                             jnp.concatenate([u, v[rw]], axis=0)], axis=1)
            for x, u, rw in zip(a2, u0, rows)]
    res4 = [_mm(x, y) for x, y in zip(rr, rhs4)]
    rp = [r_t[rw] + by_head(x[0:c, :LANES], x[c:2 * c, :LANES]) for x, rw in zip(res4, rows)]
    y0 = [by_head(x[0:c, LANES:], x[c:2 * c, LANES:]) for x in res4]
    gd = [_mm_tn(jnp.concatenate([b_t[rw] * w, k_t[rw] * w], axis=0), y)
          for rw, w, y in zip(rows, w_c, rhs4)]
    gg = [jnp.where(blockdiag, x[:, :LANES], 0.0) + jnp.where(eye, w, 0.0) for x, w in zip(gd, w_c)]
    dd = [jnp.where(blockdiag, x[:, LANES:], 0.0) for x in gd]

    hh = h_ref[...]
    ys = []
    for ch in range(nch):
        ys.append(_mm(rp[ch], hh) + y0[ch])
        hh = _mm(gg[ch], hh) + dd[ch]
    h_ref[...] = hh
    y = jnp.concatenate(ys, axis=0)

    mean = seg_sum(y) * (1.0 / hd)
    dev = y - mean
    var = seg_sum(dev * dev) * (1.0 / hd)
    yn = dev * lax.rsqrt(var + LNX_EPS) * lnw_ref[...] + lnb_ref[...]
    bonus = seg_sum(r * kmod * rk_ref[...]) * v
    o_ref[0] = yn + bonus


def _rwkv_core_call(r, k, v, lw, a, k_k, k_a, r_k, ln_w, ln_b):
    b, s, d = r.shape
    step = RWKV_STEP
    tile = pl.BlockSpec((1, step, LANES), lambda bi, hp, i: (bi, i, hp))
    par = pl.BlockSpec((1, LANES), lambda bi, hp, i: (0, hp))
    return pl.pallas_call(
        _rwkv_core_kernel,
        grid=(b, d // LANES, s // step),
        in_specs=[tile] * 5 + [par] * 5,
        out_specs=tile,
        out_shape=jax.ShapeDtypeStruct((b, s, d), F32),
        scratch_shapes=[pltpu.VMEM((LANES, LANES), F32)],
        compiler_params=_params(3),
        name="rwkv_recurrence",
    )(r, k, v, lw, a, k_k, k_a, r_k, ln_w, ln_b)


def _rope_tables(seq):
    inv = 1.0 / (ROPE_THETA ** (jnp.arange(0, HEAD_DIM, 2, dtype=F32) / HEAD_DIM))
    ang = jnp.arange(seq, dtype=F32)[:, None] * inv[None, :]
    cos, sin = jnp.cos(ang), jnp.sin(ang)
    cos = jnp.concatenate([cos, cos] * HEADS_PER_PAIR, axis=-1)
    sin = jnp.concatenate([-sin, sin] * HEADS_PER_PAIR, axis=-1)
    return cos, sin


def _ffn_weights(w_gate, w_up, conv_w, conv_b, w_down):
    d, f = w_gate.shape
    fc = FFN_FCHUNK
    nc = f // fc
    wg = w_gate.astype(BF16).reshape(d, nc, fc).transpose(1, 0, 2)
    wu = w_up.astype(BF16).reshape(d, nc, fc).transpose(1, 0, 2)
    cw = conv_w.reshape(CONV_WIDTH, nc, fc).transpose(1, 0, 2)
    cb = conv_b.reshape(nc, 1, fc)
    wd = w_down.astype(BF16)
    return wg, wu, cw, cb, wd


def kernel(x, norm_mix, norm_ffn, norm_final, attn_w_qkv, attn_w_o, rwkv_mu, rwkv_w_rkv, rwkv_w0, rwkv_w1, rwkv_w2, rwkv_a0, rwkv_a1, rwkv_a2, rwkv_g1, rwkv_g2, rwkv_k_k, rwkv_k_a, rwkv_r_k, rwkv_lnx_w, rwkv_lnx_b, rwkv_w_o, ffn_w_gate, ffn_w_up, ffn_conv_w, ffn_conv_b, ffn_w_down):
    b, s, d = x.shape
    depth = norm_mix.shape[0]
    n_mixers = 2
    row = lambda t: t.reshape(1, d)
    cos, sin = _rope_tables(s)
    h = x
    for i in range(depth):
        j = i // n_mixers
        if i % n_mixers == 0:
            w_qkv = attn_w_qkv[j].astype(BF16)
            q, k, vt, km = _qkv_call(h, row(norm_mix[i]), w_qkv[:, :d], w_qkv[:, d:2 * d],
                                     w_qkv[:, 2 * d:].T, cos, sin)
            attn = _moba_call(q, k, vt, km)
            h = _oproj_call(h, attn, None, attn_w_o[j].astype(BF16))
        else:
            bf = lambda t: t.astype(BF16)
            r, k, v, lw, a, g = _rwkv_proj_call(
                h, row(norm_mix[i]), rwkv_mu[j], bf(rwkv_w_rkv[j, 0]), bf(rwkv_w_rkv[j, 1]),
                bf(rwkv_w_rkv[j, 2]), row(rwkv_w0[j]), bf(rwkv_w1[j]), bf(rwkv_w2[j]),
                row(rwkv_a0[j]), bf(rwkv_a1[j]), bf(rwkv_a2[j]), bf(rwkv_g1[j]), bf(rwkv_g2[j]))
            y = _rwkv_core_call(r, k, v, lw, a, row(rwkv_k_k[j]), row(rwkv_k_a[j]),
                                row(rwkv_r_k[j]), row(rwkv_lnx_w[j]), row(rwkv_lnx_b[j]))
            h = _oproj_call(h, y, g, bf(rwkv_w_o[j]))
        g_final = row(norm_final) if i == depth - 1 else None
        h = _ffn_call(h, row(norm_ffn[i]),
                      *_ffn_weights(ffn_w_gate[i], ffn_w_up[i], ffn_conv_w[i], ffn_conv_b[i],
                                    ffn_w_down[i]), g_final)
    return h
```

```python
import functools
import math

import jax
import jax.numpy as jnp
from jax import lax
from jax.experimental import pallas as pl
from jax.experimental.pallas import tpu as pltpu

F32 = jnp.float32
BF16 = jnp.bfloat16
HIGHEST = lax.Precision.HIGHEST

LANES = 128
HEAD_DIM = 64
HEADS_PER_PAIR = LANES // HEAD_DIM
MOBA_BLOCK = 256
MOBA_TOPK = 3
MOBA_GROUP = 4
MOBA_HEADS = 4
ROPE_THETA = 10000.0
RMS_EPS = 1e-6
LNX_EPS = 64e-5
CONV_WIDTH = 3
RWKV_CHUNK = 64
RWKV_STEP = 2048
RWKV_CUMSUM_ROWS = 256
RWKV_PROJ_TILE = 256
FFN_FCHUNK = 256
TOKEN_TILE = 512
HALO_F32 = 8
HALO_BF16 = 16
VMEM_LIMIT = 56 * 1024 * 1024

_NT = (((1,), (1,)), ((), ()))
_TN = (((0,), (0,)), ((), ()))


def _mm(a, b):
    return jnp.dot(a.astype(BF16), b.astype(BF16), preferred_element_type=F32)


def _mm_nt(a, b):
    return lax.dot_general(a.astype(BF16), b.astype(BF16), _NT, preferred_element_type=F32)


def _mm_tn(a, b):
    return lax.dot_general(a.astype(BF16), b.astype(BF16), _TN, preferred_element_type=F32)


def _rms(x, g):
    y = x * lax.rsqrt(jnp.mean(x * x, axis=-1, keepdims=True) + RMS_EPS)
    return y * g


def _params(n_axes):
    return pltpu.CompilerParams(
        dimension_semantics=("arbitrary",) * n_axes, vmem_limit_bytes=VMEM_LIMIT)


def _const_spec(shape):
    zeros = (0,) * len(shape)
    return pl.BlockSpec(shape, lambda *_: zeros)


def _qkv_kernel(x_ref, g_ref, wq_ref, wk_ref, wvt_ref, cos_ref, sin_ref,
                q_ref, k_ref, vt_ref, km_ref, *, scale):
    ts, d = x_ref.shape[1], x_ref.shape[2]
    xb = _rms(x_ref[0], g_ref[...]).astype(BF16)
    q = jnp.dot(xb, wq_ref[...], preferred_element_type=F32)
    k = jnp.dot(xb, wk_ref[...], preferred_element_type=F32)
    vt = lax.dot_general(wvt_ref[...], xb, _NT, preferred_element_type=F32)

    reps = d // LANES
    cos = jnp.concatenate([cos_ref[...]] * reps, axis=1)
    sin = jnp.concatenate([sin_ref[...]] * reps, axis=1)
    lane = lax.broadcasted_iota(jnp.int32, (1, d), 1)
    first_half = (lane & (HEAD_DIM - 1)) < HEAD_DIM // 2

    def rope(t):
        partner = jnp.where(first_half,
                            pltpu.roll(t, d - HEAD_DIM // 2, 1),
                            pltpu.roll(t, HEAD_DIM // 2, 1))
        return t * cos + partner * sin

    q = rope(q) * scale
    k = rope(k)
    q_ref[0] = q.astype(BF16)
    k_ref[0] = k.astype(BF16)
    for r in range(ts // MOBA_BLOCK):
        rows = slice(r * MOBA_BLOCK, (r + 1) * MOBA_BLOCK)
        vt_ref[0, r] = vt[:, rows].astype(BF16)
        km_ref[0, r] = jnp.mean(k[rows], axis=0, keepdims=True)


def _qkv_call(x, g, wq, wk, wvt, cos, sin):
    b, s, d = x.shape
    ts = TOKEN_TILE
    nb = s // MOBA_BLOCK
    bpt = ts // MOBA_BLOCK
    scale = math.log2(math.e) / math.sqrt(HEAD_DIM)
    return pl.pallas_call(
        functools.partial(_qkv_kernel, scale=scale),
        grid=(b, s // ts),
        in_specs=[
            pl.BlockSpec((1, ts, d), lambda bi, i: (bi, i, 0)),
            _const_spec((1, d)),
            _const_spec((d, d)),
            _const_spec((d, d)),
            _const_spec((d, d)),
            pl.BlockSpec((ts, LANES), lambda bi, i: (i, 0)),
            pl.BlockSpec((ts, LANES), lambda bi, i: (i, 0)),
        ],
        out_specs=[
            pl.BlockSpec((1, ts, d), lambda bi, i: (bi, i, 0)),
            pl.BlockSpec((1, ts, d), lambda bi, i: (bi, i, 0)),
            pl.BlockSpec((1, bpt, d, MOBA_BLOCK), lambda bi, i: (bi, i, 0, 0)),
            pl.BlockSpec((1, bpt, 1, d), lambda bi, i: (bi, i, 0, 0)),
        ],
        out_shape=[
            jax.ShapeDtypeStruct((b, s, d), BF16),
            jax.ShapeDtypeStruct((b, s, d), BF16),
            jax.ShapeDtypeStruct((b, nb, d, MOBA_BLOCK), BF16),
            jax.ShapeDtypeStruct((b, nb, 1, d), F32),
        ],
        compiler_params=_params(2),
        name="qkv_rope",
    )(x, g, wq, wk, wvt, cos, sin)


def _moba_kernel(q_ref, k_ref, vt_ref, km_ref, o_ref, bias_ref, sa_ref, sb_ref, ca_ref, cb_ref):
    i = pl.program_id(2)
    blk = MOBA_BLOCK
    nb = km_ref.shape[1]
    neg_inf = -jnp.inf
    width = q_ref.shape[2]
    heads = range(width // HEAD_DIM)
    q = q_ref[0]
    km = km_ref[0, :, 0, :]
    lane = lax.broadcasted_iota(jnp.int32, (1, width), 1)
    jrow = lax.broadcasted_iota(jnp.int32, (nb, blk), 0)
    past = jrow < i

    in_heads = [(lane >= hh * HEAD_DIM) & (lane < (hh + 1) * HEAD_DIM) for hh in heads]
    qms = [jnp.where(in_head, q, jnp.zeros_like(q)) for in_head in in_heads]

    grp = MOBA_GROUP
    assert nb % grp == 0
    last_group = nb // grp - 1
    buf_a = (sa_ref, ca_ref)
    buf_b = (sb_ref, cb_ref)

    def scores_head(dst, g, hh):
        s_ref, cmax_ref = dst
        g = jnp.minimum(g, last_group)
        k_grp = k_ref[0, pl.ds(pl.multiple_of(g * (grp * blk), grp * blk), grp * blk), :]
        s = lax.dot_general(k_grp, qms[hh], _NT, preferred_element_type=F32).astype(BF16)
        s_ref[hh] = s
        for jj in range(grp):
            cmax_ref[hh, jj:jj + 1, :] = jnp.max(
                s[jj * blk:(jj + 1) * blk], axis=0, keepdims=True).astype(F32)

    for hh in heads:
        scores_head(buf_a, 0, hh)

    for hh in heads:
        kmm = jnp.where(in_heads[hh], km, 0.0)
        gate = lax.dot_general(kmm, qms[hh].astype(F32), _NT, precision=HIGHEST,
                               preferred_element_type=F32)
        g = jnp.where(past, gate, neg_inf)
        sel = jnp.zeros((nb, blk), jnp.bool_)
        for _ in range(MOBA_TOPK):
            m = jnp.max(g, axis=0, keepdims=True)
            idx = jnp.min(jnp.where(g == m, jrow, nb), axis=0, keepdims=True)
            pick = jrow == idx
            sel = sel | pick
            g = jnp.where(pick, neg_inf, g)
        bias = jnp.where(sel & past, 0.0, neg_inf)
        for gg in range(nb // MOBA_GROUP):
            bias_ref[hh, gg] = bias[gg * MOBA_GROUP:(gg + 1) * MOBA_GROUP]

    ones_rows = jnp.ones((HALO_BF16, blk), BF16)

    def v_aug(j, hh):
        vt = vt_ref[0, j, hh * HEAD_DIM:(hh + 1) * HEAD_DIM, :]
        return jnp.concatenate([vt, ones_rows], axis=0)

    krow = lax.broadcasted_iota(jnp.int32, (blk, blk), 0)
    qcol = lax.broadcasted_iota(jnp.int32, (blk, blk), 1)
    causal = krow <= qcol
    k_own = k_ref[0, pl.ds(pl.multiple_of(i * blk, blk), blk), :]
    state = []
    own_scores = [lax.dot_general(k_own, qm, _NT, preferred_element_type=F32) for qm in qms]
    for hh in heads:
        s = jnp.where(causal, own_scores[hh], neg_inf)
        m = jnp.max(s, axis=0, keepdims=True).astype(BF16).astype(F32)
        p = jnp.exp2((s - m).astype(BF16))
        acc = jnp.dot(v_aug(i, hh), p, preferred_element_type=F32)
        state += [m, acc]

    def consume_head(src, g, hh, m, acc):
        s_ref, cmax_ref = src
        j0 = g * grp
        bias = bias_ref[hh, g]
        m_blk = jnp.max(cmax_ref[hh] + bias, axis=0, keepdims=True)
        m_new = jnp.maximum(m, m_blk)
        alpha = jnp.exp2(m - m_new)
        shift = (bias - m_new).astype(BF16)
        p = jnp.concatenate(
            [jnp.exp2(s_ref[hh, jj * blk:(jj + 1) * blk, :] + shift[jj:jj + 1])
             for jj in range(grp)], axis=0)
        vt = jnp.concatenate([v_aug(j0 + jj, hh) for jj in range(grp)], axis=1)
        return m_new, alpha * acc + jnp.dot(vt, p, preferred_element_type=F32)

    def consume(src, g, carry):
        out = []
        for hh in heads:
            out += consume_head(src, g, hh, *carry[2 * hh:2 * hh + 2])
        return tuple(out)

    def overlapped(dst, g_next, src, g, carry):
        out = []
        for hh in heads:
            scores_head(dst, g_next, hh)
            out += consume_head(src, g, hh, *carry[2 * hh:2 * hh + 2])
        return tuple(out)

    n_groups = lax.shift_right_logical(i + (grp - 1), int(math.log2(grp)))
    n_pairs = lax.shift_right_logical(n_groups, 1)

    def body(t, carry):
        g0 = 2 * t
        carry = overlapped(buf_b, g0 + 1, buf_a, g0, carry)
        return overlapped(buf_a, g0 + 2, buf_b, g0 + 1, carry)

    state = lax.fori_loop(0, n_pairs, body, tuple(state))
    state = lax.cond((n_groups & 1) == 1,
                     lambda c: consume(buf_a, 2 * n_pairs, c), lambda c: c, state)
    outs = []
    for hh in heads:
        acc = state[2 * hh + 1]
        outs.append(acc[:HEAD_DIM] / acc[HEAD_DIM:HEAD_DIM + 1])
    o_ref[0] = jnp.concatenate(outs, axis=0).T.astype(o_ref.dtype)


def _moba_call(q, k, vt, km):
    b, s, d = q.shape
    nb = s // MOBA_BLOCK
    width = MOBA_HEADS * HEAD_DIM
    return pl.pallas_call(
        _moba_kernel,
        grid=(b, d // width, nb),
        in_specs=[
            pl.BlockSpec((1, MOBA_BLOCK, width), lambda bi, hg, i: (bi, i, hg)),
            pl.BlockSpec((1, s, width), lambda bi, hg, i: (bi, 0, hg)),
            pl.BlockSpec((1, nb, width, MOBA_BLOCK), lambda bi, hg, i: (bi, 0, hg, 0)),
            pl.BlockSpec((1, nb, 1, width), lambda bi, hg, i: (bi, 0, 0, hg)),
        ],
        out_specs=pl.BlockSpec((1, MOBA_BLOCK, width), lambda bi, hg, i: (bi, i, hg)),
        out_shape=jax.ShapeDtypeStruct((b, s, d), BF16),
        scratch_shapes=[pltpu.VMEM((MOBA_HEADS, nb // MOBA_GROUP, MOBA_GROUP, MOBA_BLOCK), F32)]
                       + [pltpu.VMEM((MOBA_HEADS, MOBA_GROUP * MOBA_BLOCK, MOBA_BLOCK), BF16)] * 2
                       + [pltpu.VMEM((MOBA_HEADS, MOBA_GROUP, MOBA_BLOCK), F32)] * 2,
        compiler_params=_params(3),
        name="moba_attention",
    )(q, k, vt, km)


def _oproj_kernel(*refs, gated):
    if gated:
        h_ref, y_ref, g_ref, w_ref, o_ref = refs
        y = y_ref[0].astype(F32) * g_ref[0]
    else:
        h_ref, y_ref, w_ref, o_ref = refs
        y = y_ref[0]
    o_ref[0] = h_ref[0] + jnp.dot(y.astype(BF16), w_ref[...], preferred_element_type=F32)


def _oproj_call(h, y, gate, w):
    b, s, d = h.shape
    ts = TOKEN_TILE
    tile = pl.BlockSpec((1, ts, d), lambda bi, i: (bi, i, 0))
    gated = gate is not None
    args = (h, y, gate, w) if gated else (h, y, w)
    return pl.pallas_call(
        functools.partial(_oproj_kernel, gated=gated),
        grid=(b, s // ts),
        in_specs=[tile] * (len(args) - 1) + [_const_spec((d, d))],
        out_specs=tile,
        out_shape=jax.ShapeDtypeStruct((b, s, d), F32),
        compiler_params=_params(2),
        name="out_proj_gated" if gated else "out_proj",
    )(*args)


def _ffn_kernel(*refs, final):
    if final:
        (x_ref, halo_ref, g_ref, wg_ref, wu_ref, cw_ref, cb_ref, wd_ref, gf_ref,
         o_ref, hn_ref, gx_ref, act_ref) = refs
    else:
        (x_ref, halo_ref, g_ref, wg_ref, wu_ref, cw_ref, cb_ref, wd_ref,
         o_ref, hn_ref, gx_ref, act_ref) = refs
    i = pl.program_id(1)
    ts = x_ref.shape[1]
    nc, _, fc = wg_ref.shape
    halo = HALO_BF16
    x = x_ref[0]
    g = g_ref[...]
    hn_ref[0:halo, :] = _rms(halo_ref[0], g).astype(BF16)
    hn_ref[halo:, :] = _rms(x, g).astype(BF16)
    halo_keep = jnp.where(i == 0, 0.0, 1.0)

    for c in range(nc):
        gx = gx_ref.at[c % 2]
        gate = jnp.dot(hn_ref[...], wg_ref[c], preferred_element_type=F32)
        gx[0:halo, :] = gate[0:halo] * halo_keep
        gx[halo:, :] = gate[halo:]
        up = jnp.dot(hn_ref[halo:, :], wu_ref[c], preferred_element_type=F32)
        cw = cw_ref[c]
        conv = (gx[halo - 2:halo - 2 + ts, :] * cw[0:1]
                + gx[halo - 1:halo - 1 + ts, :] * cw[1:2]
                + gate[halo:] * cw[2:3]
                + cb_ref[c])
        act_ref[:, c * fc:(c + 1) * fc] = (conv * jax.nn.sigmoid(conv) * up).astype(BF16)

    out = x + jnp.dot(act_ref[...], wd_ref[...], preferred_element_type=F32)
    if final:
        out = _rms(out, gf_ref[...])
    o_ref[0] = out


def _ffn_call(h, g, wg, wu, cw, cb, wd, g_final):
    b, s, d = h.shape
    ts = TOKEN_TILE
    nc, _, fc = wg.shape
    halo = HALO_BF16
    hpt = ts // halo
    final = g_final is not None
    tile = pl.BlockSpec((1, ts, d), lambda bi, i: (bi, i, 0))
    in_specs = [
        tile,
        pl.BlockSpec((1, halo, d), lambda bi, i: (bi, jnp.maximum(i * hpt - 1, 0), 0)),
        _const_spec((1, d)),
        _const_spec((nc, d, fc)),
        _const_spec((nc, d, fc)),
        _const_spec((nc, CONV_WIDTH, fc)),
        _const_spec((nc, 1, fc)),
        _const_spec((nc * fc, d)),
    ]
    args = [h, h, g, wg, wu, cw, cb, wd]
    if final:
        in_specs.append(_const_spec((1, d)))
        args.append(g_final)
    return pl.pallas_call(
        functools.partial(_ffn_kernel, final=final),
        grid=(b, s // ts),
        in_specs=in_specs,
        out_specs=tile,
        out_shape=jax.ShapeDtypeStruct((b, s, d), F32),
        scratch_shapes=[
            pltpu.VMEM((ts + halo, d), BF16),
            pltpu.VMEM((2, ts + halo, fc), F32),
            pltpu.VMEM((ts, nc * fc), BF16),
        ],
        compiler_params=_params(2),
        name="conv_ffn_final" if final else "conv_ffn",
    )(*args)


def _rwkv_proj_kernel(x_ref, halo_ref, gm_ref, mu_ref, wr_ref, wk_ref, wv_ref,
                      w0_ref, w1_ref, w2_ref, a0_ref, a1_ref, a2_ref, g1_ref, g2_ref,
                      r_ref, k_ref, v_ref, lw_ref, a_ref, g_ref, xs_ref):
    i = pl.program_id(1)
    ts = x_ref.shape[1]
    halo = HALO_F32
    gm = gm_ref[...]
    xn = _rms(x_ref[0], gm)
    xs_ref[0:halo, :] = _rms(halo_ref[0], gm)
    xs_ref[halo:, :] = xn

    @pl.when(i == 0)
    def _():
        xs_ref[0:halo, :] = jnp.zeros((halo, xs_ref.shape[1]), F32)

    xx = xs_ref[halo - 1:halo - 1 + ts, :] - xn

    def mix(n):
        return (xn + xx * mu_ref[n:n + 1, :]).astype(BF16)

    r_ref[0] = jnp.dot(mix(0), wr_ref[...], preferred_element_type=F32)
    k_ref[0] = jnp.dot(mix(2), wk_ref[...], preferred_element_type=F32)
    v_ref[0] = jnp.dot(mix(3), wv_ref[...], preferred_element_type=F32)

    wl = w0_ref[...] + _mm(jnp.tanh(jnp.dot(mix(1), w1_ref[...], preferred_element_type=F32)),
                           w2_ref[...])
    z = -wl
    softplus = jnp.maximum(z, 0.0) + jnp.log(1.0 + jnp.exp(-jnp.abs(z)))
    w = -softplus - 0.5
    lw_ref[0] = -jnp.exp(w)

    al = a0_ref[...] + _mm(jnp.dot(mix(4), a1_ref[...], preferred_element_type=F32), a2_ref[...])
    a_ref[0] = jax.nn.sigmoid(al)
    g_ref[0] = _mm(jax.nn.sigmoid(jnp.dot(mix(5), g1_ref[...], preferred_element_type=F32)),
                   g2_ref[...])


def _rwkv_proj_call(h, gm, mu, wr, wk, wv, w0, w1, w2, a0, a1, a2, g1, g2):
    b, s, d = h.shape
    ts = RWKV_PROJ_TILE
    halo = HALO_F32
    hpt = ts // halo
    tile = pl.BlockSpec((1, ts, d), lambda bi, i: (bi, i, 0))
    consts = [gm, mu, wr, wk, wv, w0, w1, w2, a0, a1, a2, g1, g2]
    return pl.pallas_call(
        _rwkv_proj_kernel,
        grid=(b, s // ts),
        in_specs=[tile,
                  pl.BlockSpec((1, halo, d), lambda bi, i: (bi, jnp.maximum(i * hpt - 1, 0), 0))]
                 + [_const_spec(c.shape) for c in consts],
        out_specs=[tile] * 6,
        out_shape=[jax.ShapeDtypeStruct((b, s, d), F32)] * 6,
        scratch_shapes=[pltpu.VMEM((ts + halo, d), F32)],
        compiler_params=_params(2),
        name="rwkv_proj",
    )(h, h, *consts)


def _rwkv_core_kernel(r_ref, k_ref, v_ref, lw_ref, a_ref, kk_ref, ka_ref, rk_ref,
                      lnw_ref, lnb_ref, o_ref, h_ref):
    c = RWKV_CHUNK
    hd = HEAD_DIM
    c_shift = int(math.log2(c))
    step = r_ref.shape[1]
    nch = step // c

    @pl.when(pl.program_id(2) == 0)
    def _():
        h_ref[...] = jnp.zeros_like(h_ref)

    lane = lax.broadcasted_iota(jnp.int32, (1, LANES), 1)
    head0 = lane < hd

    def by_head(x0, x1):
        return jnp.where(head0, x0, x1)

    def seg_sum(t):
        s0 = jnp.sum(jnp.where(head0, t, 0.0), axis=-1, keepdims=True)
        s1 = jnp.sum(jnp.where(head0, 0.0, t), axis=-1, keepdims=True)
        return by_head(s0, s1)

    r, k, v, lw, a = r_ref[0], k_ref[0], v_ref[0], lw_ref[0], a_ref[0]

    sub = min(step, RWKV_CUMSUM_ROWS)
    tr = lax.broadcasted_iota(jnp.int32, (sub, sub), 0)
    tc = lax.broadcasted_iota(jnp.int32, (sub, sub), 1)
    same_chunk = lax.shift_right_logical(tr, c_shift) == lax.shift_right_logical(tc, c_shift)
    ltri = jnp.where((tr >= tc) & same_chunk, 1.0, 0.0).astype(BF16)
    hi = lw.astype(BF16)
    rem = lw - hi.astype(F32)
    mid = rem.astype(BF16)
    lo = (rem - mid.astype(F32)).astype(BF16)
    split = jnp.concatenate([hi, mid, lo], axis=1)
    cum3 = jnp.concatenate(
        [jnp.dot(ltri, split[n * sub:(n + 1) * sub], preferred_element_type=F32)
         for n in range(step // sub)], axis=0)
    cum = cum3[:, :LANES] + cum3[:, LANES:2 * LANES] + cum3[:, 2 * LANES:]

    e_w = jnp.exp(cum)
    e_wi = jnp.exp(-cum)
    e_wp = jnp.exp(cum - lw)
    kk = k * kk_ref[...]
    kk = kk / jnp.maximum(jnp.sqrt(seg_sum(kk * kk)), 1e-12)
    kmod = k * (1.0 + (a - 1.0) * ka_ref[...])
    a_t = -kk * e_wp
    b_t = kk * a * e_wi
    k_t = kmod * e_wi
    r_t = r * e_w

    rows = [slice(ch * c, (ch + 1) * c) for ch in range(nch)]
    w_c = [e_w[ch * c + c - 1:ch * c + c, :] for ch in range(nch)]
    zeros_c = jnp.zeros((c, LANES), F32)

    r4 = lax.broadcasted_iota(jnp.int32, (4 * c, LANES), 0)
    l4 = lax.broadcasted_iota(jnp.int32, (4 * c, LANES), 1)
    t_idx = r4 & (c - 1)
    s_idx = l4 & (c - 1)
    incl = lax.shift_right_logical(r4, c_shift) & 1
    tri_mask = t_idx - s_idx + incl > 0
    r2 = lax.broadcasted_iota(jnp.int32, (LANES, LANES), 0)
    l2 = lax.broadcasted_iota(jnp.int32, (LANES, LANES), 1)
    eye = r2 == l2
    blockdiag = (r2 < hd) == (l2 < hd)
    eye_f = jnp.where(eye, 1.0, 0.0)

    aa, rr = [], []
    for rw in rows:
        a_c, r_c = a_t[rw], r_t[rw]
        lhs = jnp.concatenate([by_head(a_c, 0.0), by_head(r_c, 0.0),
                               by_head(0.0, a_c), by_head(0.0, r_c)], axis=0)
        rhs = jnp.concatenate([b_t[rw], k_t[rw]], axis=0)
        res = jnp.where(tri_mask, _mm_nt(lhs, rhs), 0.0)
        aa.append(jnp.concatenate([res[0:c], res[2 * c:3 * c]], axis=0))
        rr.append(jnp.concatenate([res[c:2 * c], res[3 * c:4 * c]], axis=0))

    rp, y0, gg, dd = ([None] * nch for _ in range(4))

    def parallel_stages(chunks, tick):
        rws = [rows[ch] for ch in chunks]
        wcs = [w_c[ch] for ch in chunks]
        aas = [aa[ch] for ch in chunks]
        n_bd = [jnp.where(blockdiag, by_head(x, pltpu.roll(x, hd, 1)), 0.0) for x in aas]
        inv = [eye_f + n for n in n_bd]
        p = n_bd
        for _ in range(c_shift - 1):
            p = [_mm(x, x) for x in p]
            tick()
            inv = [t + _mm(t, x) for t, x in zip(inv, p)]
            tick()
        akv = [_mm(x, jnp.concatenate([zeros_c, v[rw]], axis=0)) for x, rw in zip(aas, rws)]
        tick()
        ua = [_mm(t, jnp.concatenate([x, jnp.concatenate([a_t[rw], a_t[rw]], axis=0)], axis=1))
              for t, x, rw in zip(inv, akv, rws)]
        tick()
        u0 = [by_head(x[0:c, :LANES], x[c:2 * c, :LANES]) for x in ua]
        a2 = [by_head(x[0:c, LANES:], x[c:2 * c, LANES:]) for x in ua]
        rhs4 = [jnp.concatenate([jnp.concatenate([x, zeros_c], axis=0),
                                 jnp.concatenate([u, v[rw]], axis=0)], axis=1)
                for x, u, rw in zip(a2, u0, rws)]
        res4 = [_mm(rr[ch], y) for ch, y in zip(chunks, rhs4)]
        tick()
        gd = [_mm_tn(jnp.concatenate([b_t[rw] * w, k_t[rw] * w], axis=0), y)
              for rw, w, y in zip(rws, wcs, rhs4)]
        tick()
        for n, ch in enumerate(chunks):
            x = res4[n]
            rp[ch] = r_t[rws[n]] + by_head(x[0:c, :LANES], x[c:2 * c, :LANES])
            y0[ch] = by_head(x[0:c, LANES:], x[c:2 * c, LANES:])
            gg[ch] = jnp.where(blockdiag, gd[n][:, :LANES], 0.0) + jnp.where(eye, wcs[n], 0.0)
            dd[ch] = jnp.where(blockdiag, gd[n][:, LANES:], 0.0)

    chain = {"hh": h_ref[...], "todo": []}
    ys = [None] * nch

    def chain_step():
        if chain["todo"]:
            ch = chain["todo"].pop(0)
            ys[ch] = _mm(rp[ch], chain["hh"]) + y0[ch]
            chain["hh"] = _mm(gg[ch], chain["hh"]) + dd[ch]

    first = list(range(nch // 2))
    second = list(range(nch // 2, nch))
    parallel_stages(first, lambda: None)
    chain["todo"] += first
    parallel_stages(second, chain_step)
    chain["todo"] += second
    while chain["todo"]:
        chain_step()
    h_ref[...] = chain["hh"]
    y = jnp.concatenate(ys, axis=0)

    mean = seg_sum(y) * (1.0 / hd)
    dev = y - mean
    var = seg_sum(dev * dev) * (1.0 / hd)
    yn = dev * lax.rsqrt(var + LNX_EPS) * lnw_ref[...] + lnb_ref[...]
    bonus = seg_sum(r * kmod * rk_ref[...]) * v
    o_ref[0] = yn + bonus


def _rwkv_core_call(r, k, v, lw, a, k_k, k_a, r_k, ln_w, ln_b):
    b, s, d = r.shape
    step = RWKV_STEP
    tile = pl.BlockSpec((1, step, LANES), lambda bi, hp, i: (bi, i, hp))
    par = pl.BlockSpec((1, LANES), lambda bi, hp, i: (0, hp))
    return pl.pallas_call(
        _rwkv_core_kernel,
        grid=(b, d // LANES, s // step),
        in_specs=[tile] * 5 + [par] * 5,
        out_specs=tile,
        out_shape=jax.ShapeDtypeStruct((b, s, d), F32),
        scratch_shapes=[pltpu.VMEM((LANES, LANES), F32)],
        compiler_params=_params(3),
        name="rwkv_recurrence",
    )(r, k, v, lw, a, k_k, k_a, r_k, ln_w, ln_b)


def _rope_tables(seq):
    inv = 1.0 / (ROPE_THETA ** (jnp.arange(0, HEAD_DIM, 2, dtype=F32) / HEAD_DIM))
    ang = jnp.arange(seq, dtype=F32)[:, None] * inv[None, :]
    cos, sin = jnp.cos(ang), jnp.sin(ang)
    cos = jnp.concatenate([cos, cos] * HEADS_PER_PAIR, axis=-1)
    sin = jnp.concatenate([-sin, sin] * HEADS_PER_PAIR, axis=-1)
    return cos, sin


def _ffn_weights(w_gate, w_up, conv_w, conv_b, w_down):
    d, f = w_gate.shape
    fc = FFN_FCHUNK
    nc = f // fc
    wg = w_gate.astype(BF16).reshape(d, nc, fc).transpose(1, 0, 2)
    wu = w_up.astype(BF16).reshape(d, nc, fc).transpose(1, 0, 2)
    cw = conv_w.reshape(CONV_WIDTH, nc, fc).transpose(1, 0, 2)
    cb = conv_b.reshape(nc, 1, fc)
    wd = w_down.astype(BF16)
    return wg, wu, cw, cb, wd


def kernel(x, norm_mix, norm_ffn, norm_final, attn_w_qkv, attn_w_o, rwkv_mu, rwkv_w_rkv, rwkv_w0, rwkv_w1, rwkv_w2, rwkv_a0, rwkv_a1, rwkv_a2, rwkv_g1, rwkv_g2, rwkv_k_k, rwkv_k_a, rwkv_r_k, rwkv_lnx_w, rwkv_lnx_b, rwkv_w_o, ffn_w_gate, ffn_w_up, ffn_conv_w, ffn_conv_b, ffn_w_down):
    b, s, d = x.shape
    depth = norm_mix.shape[0]
    n_mixers = 2
    row = lambda t: t.reshape(1, d)
    cos, sin = _rope_tables(s)
    h = x
    for i in range(depth):
        j = i // n_mixers
        if i % n_mixers == 0:
            w_qkv = attn_w_qkv[j].astype(BF16)
            q, k, vt, km = _qkv_call(h, row(norm_mix[i]), w_qkv[:, :d], w_qkv[:, d:2 * d],
                                     w_qkv[:, 2 * d:].T, cos, sin)
            attn = _moba_call(q, k, vt, km)
            h = _oproj_call(h, attn, None, attn_w_o[j].astype(BF16))
        else:
            bf = lambda t: t.astype(BF16)
            r, k, v, lw, a, g = _rwkv_proj_call(
                h, row(norm_mix[i]), rwkv_mu[j], bf(rwkv_w_rkv[j, 0]), bf(rwkv_w_rkv[j, 1]),
                bf(rwkv_w_rkv[j, 2]), row(rwkv_w0[j]), bf(rwkv_w1[j]), bf(rwkv_w2[j]),
                row(rwkv_a0[j]), bf(rwkv_a1[j]), bf(rwkv_a2[j]), bf(rwkv_g1[j]), bf(rwkv_g2[j]))
            y = _rwkv_core_call(r, k, v, lw, a, row(rwkv_k_k[j]), row(rwkv_k_a[j]),
                                row(rwkv_r_k[j]), row(rwkv_lnx_w[j]), row(rwkv_lnx_b[j]))
            h = _oproj_call(h, y, g, bf(rwkv_w_o[j]))
        g_final = row(norm_final) if i == depth - 1 else None
        h = _ffn_call(h, row(norm_ffn[i]),
                      *_ffn_weights(ffn_w_gate[i], ffn_w_up[i], ffn_conv_w[i], ffn_conv_b[i],
                                    ffn_w_down[i]), g_final)
    return h
```

```python
import functools
import math

import jax
import jax.numpy as jnp
from jax import lax
from jax.experimental import pallas as pl
from jax.experimental.pallas import tpu as pltpu

F32 = jnp.float32
BF16 = jnp.bfloat16
HIGHEST = lax.Precision.HIGHEST

LANES = 128
HEAD_DIM = 64
HEADS_PER_PAIR = LANES // HEAD_DIM
MOBA_BLOCK = 256
MOBA_TOPK = 3
MOBA_GROUP = 4
MOBA_HEADS = 4
ROPE_THETA = 10000.0
RMS_EPS = 1e-6
LNX_EPS = 64e-5
CONV_WIDTH = 3
RWKV_CHUNK = 64
RWKV_STEP = 2048
RWKV_CUMSUM_ROWS = 256
RWKV_PROJ_TILE = 256
FFN_FCHUNK = 256
TOKEN_TILE = 512
HALO_F32 = 8
HALO_BF16 = 16
VMEM_LIMIT = 56 * 1024 * 1024

_NT = (((1,), (1,)), ((), ()))
_TN = (((0,), (0,)), ((), ()))


def _mm(a, b):
    return jnp.dot(a.astype(BF16), b.astype(BF16), preferred_element_type=F32)


def _mm_nt(a, b):
    return lax.dot_general(a.astype(BF16), b.astype(BF16), _NT, preferred_element_type=F32)


def _mm_tn(a, b):
    return lax.dot_general(a.astype(BF16), b.astype(BF16), _TN, preferred_element_type=F32)


def _rms(x, g):
    y = x * lax.rsqrt(jnp.mean(x * x, axis=-1, keepdims=True) + RMS_EPS)
    return y * g


def _params(n_axes):
    return pltpu.CompilerParams(
        dimension_semantics=("arbitrary",) * n_axes, vmem_limit_bytes=VMEM_LIMIT)


def _const_spec(shape):
    zeros = (0,) * len(shape)
    return pl.BlockSpec(shape, lambda *_: zeros)


def _qkv_kernel(x_ref, g_ref, wq_ref, wk_ref, wvt_ref, cos_ref, sin_ref,
                q_ref, k_ref, vt_ref, km_ref, *, scale):
    ts, d = x_ref.shape[1], x_ref.shape[2]
    xb = _rms(x_ref[0], g_ref[...]).astype(BF16)
    q = jnp.dot(xb, wq_ref[...], preferred_element_type=F32)
    k = jnp.dot(xb, wk_ref[...], preferred_element_type=F32)
    vt = lax.dot_general(wvt_ref[...], xb, _NT, preferred_element_type=F32)

    reps = d // LANES
    cos = jnp.concatenate([cos_ref[...]] * reps, axis=1)
    sin = jnp.concatenate([sin_ref[...]] * reps, axis=1)
    lane = lax.broadcasted_iota(jnp.int32, (1, d), 1)
    first_half = (lane & (HEAD_DIM - 1)) < HEAD_DIM // 2

    def rope(t):
        partner = jnp.where(first_half,
                            pltpu.roll(t, d - HEAD_DIM // 2, 1),
                            pltpu.roll(t, HEAD_DIM // 2, 1))
        return t * cos + partner * sin

    q = rope(q) * scale
    k = rope(k)
    q_ref[0] = q.astype(BF16)
    k_ref[0] = k.astype(BF16)
    for r in range(ts // MOBA_BLOCK):
        rows = slice(r * MOBA_BLOCK, (r + 1) * MOBA_BLOCK)
        vt_ref[0, r] = vt[:, rows].astype(BF16)
        km_ref[0, r] = jnp.mean(k[rows], axis=0, keepdims=True)


def _qkv_call(x, g, wq, wk, wvt, cos, sin):
    b, s, d = x.shape
    ts = TOKEN_TILE
    nb = s // MOBA_BLOCK
    bpt = ts // MOBA_BLOCK
    scale = math.log2(math.e) / math.sqrt(HEAD_DIM)
    return pl.pallas_call(
        functools.partial(_qkv_kernel, scale=scale),
        grid=(b, s // ts),
        in_specs=[
            pl.BlockSpec((1, ts, d), lambda bi, i: (bi, i, 0)),
            _const_spec((1, d)),
            _const_spec((d, d)),
            _const_spec((d, d)),
            _const_spec((d, d)),
            pl.BlockSpec((ts, LANES), lambda bi, i: (i, 0)),
            pl.BlockSpec((ts, LANES), lambda bi, i: (i, 0)),
        ],
        out_specs=[
            pl.BlockSpec((1, ts, d), lambda bi, i: (bi, i, 0)),
            pl.BlockSpec((1, ts, d), lambda bi, i: (bi, i, 0)),
            pl.BlockSpec((1, bpt, d, MOBA_BLOCK), lambda bi, i: (bi, i, 0, 0)),
            pl.BlockSpec((1, bpt, 1, d), lambda bi, i: (bi, i, 0, 0)),
        ],
        out_shape=[
            jax.ShapeDtypeStruct((b, s, d), BF16),
            jax.ShapeDtypeStruct((b, s, d), BF16),
            jax.ShapeDtypeStruct((b, nb, d, MOBA_BLOCK), BF16),
            jax.ShapeDtypeStruct((b, nb, 1, d), F32),
        ],
        compiler_params=_params(2),
        name="qkv_rope",
    )(x, g, wq, wk, wvt, cos, sin)


def _moba_kernel(q_ref, k_ref, vt_ref, km_ref, o_ref, bias_ref, sa_ref, sb_ref, ca_ref, cb_ref):
    i = pl.program_id(2)
    blk = MOBA_BLOCK
    nb = km_ref.shape[1]
    neg_inf = -jnp.inf
    width = q_ref.shape[2]
    heads = range(width // HEAD_DIM)
    q = q_ref[0]
    km = km_ref[0, :, 0, :]
    lane = lax.broadcasted_iota(jnp.int32, (1, width), 1)
    jrow = lax.broadcasted_iota(jnp.int32, (nb, blk), 0)
    past = jrow < i

    in_heads = [(lane >= hh * HEAD_DIM) & (lane < (hh + 1) * HEAD_DIM) for hh in heads]
    qms = [jnp.where(in_head, q, jnp.zeros_like(q)) for in_head in in_heads]

    grp = MOBA_GROUP
    assert nb % grp == 0
    last_group = nb // grp - 1
    buf_a = (sa_ref, ca_ref)
    buf_b = (sb_ref, cb_ref)

    def store_scores(dst, hh, s):
        s_ref, cmax_ref = dst
        s_ref[hh] = s
        for jj in range(grp):
            cmax_ref[hh, jj:jj + 1, :] = jnp.max(
                s[jj * blk:(jj + 1) * blk], axis=0, keepdims=True).astype(F32)

    def scores_head(dst, g, hh):
        g = jnp.minimum(g, last_group)
        k_grp = k_ref[0, pl.ds(pl.multiple_of(g * (grp * blk), grp * blk), grp * blk), :]
        store_scores(dst, hh, lax.dot_general(k_grp, qms[hh], _NT,
                                              preferred_element_type=F32).astype(BF16))

    km_hi = km.astype(BF16)
    km_rem = km - km_hi.astype(F32)
    km_mid = km_rem.astype(BF16)
    km_lo = (km_rem - km_mid.astype(F32)).astype(BF16)
    lhs0 = jnp.concatenate([k_ref[0, 0:grp * blk, :], km_hi, km_mid, km_lo], axis=0)
    n0 = grp * blk

    for hh in heads:
        r0 = lax.dot_general(lhs0, qms[hh], _NT, preferred_element_type=F32)
        store_scores(buf_a, hh, r0[:n0].astype(BF16))
        gate = r0[n0:n0 + nb] + r0[n0 + nb:n0 + 2 * nb] + r0[n0 + 2 * nb:]
        g = jnp.where(past, gate, neg_inf)
        sel = jnp.zeros((nb, blk), jnp.bool_)
        for _ in range(MOBA_TOPK):
            m = jnp.max(g, axis=0, keepdims=True)
            idx = jnp.min(jnp.where(g == m, jrow, nb), axis=0, keepdims=True)
            pick = jrow == idx
            sel = sel | pick
            g = jnp.where(pick, neg_inf, g)
        bias = jnp.where(sel & past, 0.0, neg_inf)
        for gg in range(nb // MOBA_GROUP):
            bias_ref[hh, gg] = bias[gg * MOBA_GROUP:(gg + 1) * MOBA_GROUP]

    ones_rows = jnp.ones((HALO_BF16, blk), BF16)

    def v_aug(j, hh):
        vt = vt_ref[0, j, hh * HEAD_DIM:(hh + 1) * HEAD_DIM, :]
        return jnp.concatenate([vt, ones_rows], axis=0)

    krow = lax.broadcasted_iota(jnp.int32, (blk, blk), 0)
    qcol = lax.broadcasted_iota(jnp.int32, (blk, blk), 1)
    causal = krow <= qcol
    k_own = k_ref[0, pl.ds(pl.multiple_of(i * blk, blk), blk), :]
    state = []
    own_scores = [lax.dot_general(k_own, qm, _NT, preferred_element_type=F32) for qm in qms]
    for hh in heads:
        s = jnp.where(causal, own_scores[hh], neg_inf)
        m = jnp.max(s, axis=0, keepdims=True).astype(BF16).astype(F32)
        p = jnp.exp2((s - m).astype(BF16))
        acc = jnp.dot(v_aug(i, hh), p, preferred_element_type=F32)
        state += [m, acc]

    def consume_head(src, g, hh, m, acc):
        s_ref, cmax_ref = src
        j0 = g * grp
        bias = bias_ref[hh, g]
        m_blk = jnp.max(cmax_ref[hh] + bias, axis=0, keepdims=True)
        m_new = jnp.maximum(m, m_blk)
        alpha = jnp.exp2(m - m_new)
        shift = (bias - m_new).astype(BF16)
        p = jnp.concatenate(
            [jnp.exp2(s_ref[hh, jj * blk:(jj + 1) * blk, :] + shift[jj:jj + 1])
             for jj in range(grp)], axis=0)
        vt = jnp.concatenate([v_aug(j0 + jj, hh) for jj in range(grp)], axis=1)
        return m_new, alpha * acc + jnp.dot(vt, p, preferred_element_type=F32)

    def consume(src, g, carry):
        out = []
        for hh in heads:
            out += consume_head(src, g, hh, *carry[2 * hh:2 * hh + 2])
        return tuple(out)

    def overlapped(dst, g_next, src, g, carry):
        out = []
        for hh in heads:
            scores_head(dst, g_next, hh)
            out += consume_head(src, g, hh, *carry[2 * hh:2 * hh + 2])
        return tuple(out)

    n_groups = lax.shift_right_logical(i + (grp - 1), int(math.log2(grp)))
    n_pairs = lax.shift_right_logical(n_groups, 1)

    def body(t, carry):
        g0 = 2 * t
        carry = overlapped(buf_b, g0 + 1, buf_a, g0, carry)
        return overlapped(buf_a, g0 + 2, buf_b, g0 + 1, carry)

    state = lax.fori_loop(0, n_pairs, body, tuple(state))
    state = lax.cond((n_groups & 1) == 1,
                     lambda c: consume(buf_a, 2 * n_pairs, c), lambda c: c, state)
    outs = []
    for hh in heads:
        acc = state[2 * hh + 1]
        outs.append(acc[:HEAD_DIM] / acc[HEAD_DIM:HEAD_DIM + 1])
    o_ref[0] = jnp.concatenate(outs, axis=0).T.astype(o_ref.dtype)


def _moba_call(q, k, vt, km):
    b, s, d = q.shape
    nb = s // MOBA_BLOCK
    width = MOBA_HEADS * HEAD_DIM
    return pl.pallas_call(
        _moba_kernel,
        grid=(b, d // width, nb),
        in_specs=[
            pl.BlockSpec((1, MOBA_BLOCK, width), lambda bi, hg, i: (bi, i, hg)),
            pl.BlockSpec((1, s, width), lambda bi, hg, i: (bi, 0, hg)),
            pl.BlockSpec((1, nb, width, MOBA_BLOCK), lambda bi, hg, i: (bi, 0, hg, 0)),
            pl.BlockSpec((1, nb, 1, width), lambda bi, hg, i: (bi, 0, 0, hg)),
        ],
        out_specs=pl.BlockSpec((1, MOBA_BLOCK, width), lambda bi, hg, i: (bi, i, hg)),
        out_shape=jax.ShapeDtypeStruct((b, s, d), BF16),
        scratch_shapes=[pltpu.VMEM((MOBA_HEADS, nb // MOBA_GROUP, MOBA_GROUP, MOBA_BLOCK), F32)]
                       + [pltpu.VMEM((MOBA_HEADS, MOBA_GROUP * MOBA_BLOCK, MOBA_BLOCK), BF16)] * 2
                       + [pltpu.VMEM((MOBA_HEADS, MOBA_GROUP, MOBA_BLOCK), F32)] * 2,
        compiler_params=_params(3),
        name="moba_attention",
    )(q, k, vt, km)


def _oproj_kernel(*refs, gated):
    if gated:
        h_ref, y_ref, g_ref, w_ref, o_ref = refs
        y = y_ref[0].astype(F32) * g_ref[0]
    else:
        h_ref, y_ref, w_ref, o_ref = refs
        y = y_ref[0]
    o_ref[0] = h_ref[0] + jnp.dot(y.astype(BF16), w_ref[...], preferred_element_type=F32)


def _oproj_call(h, y, gate, w):
    b, s, d = h.shape
    ts = TOKEN_TILE
    tile = pl.BlockSpec((1, ts, d), lambda bi, i: (bi, i, 0))
    gated = gate is not None
    args = (h, y, gate, w) if gated else (h, y, w)
    return pl.pallas_call(
        functools.partial(_oproj_kernel, gated=gated),
        grid=(b, s // ts),
        in_specs=[tile] * (len(args) - 1) + [_const_spec((d, d))],
        out_specs=tile,
        out_shape=jax.ShapeDtypeStruct((b, s, d), F32),
        compiler_params=_params(2),
        name="out_proj_gated" if gated else "out_proj",
    )(*args)


def _ffn_kernel(*refs, final):
    if final:
        (x_ref, halo_ref, g_ref, wg_ref, wu_ref, cw_ref, cb_ref, wd_ref, gf_ref,
         o_ref, hn_ref, gx_ref, act_ref) = refs
    else:
        (x_ref, halo_ref, g_ref, wg_ref, wu_ref, cw_ref, cb_ref, wd_ref,
         o_ref, hn_ref, gx_ref, act_ref) = refs
    i = pl.program_id(1)
    ts = x_ref.shape[1]
    nc, _, fc = wg_ref.shape
    halo = HALO_BF16
    x = x_ref[0]
    g = g_ref[...]
    hn_ref[0:halo, :] = _rms(halo_ref[0], g).astype(BF16)
    hn_ref[halo:, :] = _rms(x, g).astype(BF16)
    halo_keep = jnp.where(i == 0, 0.0, 1.0)

    for c in range(nc):
        gx = gx_ref.at[c % 2]
        gate = jnp.dot(hn_ref[...], wg_ref[c], preferred_element_type=F32)
        gx[0:halo, :] = gate[0:halo] * halo_keep
        gx[halo:, :] = gate[halo:]
        up = jnp.dot(hn_ref[halo:, :], wu_ref[c], preferred_element_type=F32)
        cw = cw_ref[c]
        conv = (gx[halo - 2:halo - 2 + ts, :] * cw[0:1]
                + gx[halo - 1:halo - 1 + ts, :] * cw[1:2]
                + gate[halo:] * cw[2:3]
                + cb_ref[c])
        act_ref[:, c * fc:(c + 1) * fc] = (conv * jax.nn.sigmoid(conv) * up).astype(BF16)

    out = x + jnp.dot(act_ref[...], wd_ref[...], preferred_element_type=F32)
    if final:
        out = _rms(out, gf_ref[...])
    o_ref[0] = out


def _ffn_call(h, g, wg, wu, cw, cb, wd, g_final):
    b, s, d = h.shape
    ts = TOKEN_TILE
    nc, _, fc = wg.shape
    halo = HALO_BF16
    hpt = ts // halo
    final = g_final is not None
    tile = pl.BlockSpec((1, ts, d), lambda bi, i: (bi, i, 0))
    in_specs = [
        tile,
        pl.BlockSpec((1, halo, d), lambda bi, i: (bi, jnp.maximum(i * hpt - 1, 0), 0)),
        _const_spec((1, d)),
        _const_spec((nc, d, fc)),
        _const_spec((nc, d, fc)),
        _const_spec((nc, CONV_WIDTH, fc)),
        _const_spec((nc, 1, fc)),
        _const_spec((nc * fc, d)),
    ]
    args = [h, h, g, wg, wu, cw, cb, wd]
    if final:
        in_specs.append(_const_spec((1, d)))
        args.append(g_final)
    return pl.pallas_call(
        functools.partial(_ffn_kernel, final=final),
        grid=(b, s // ts),
        in_specs=in_specs,
        out_specs=tile,
        out_shape=jax.ShapeDtypeStruct((b, s, d), F32),
        scratch_shapes=[
            pltpu.VMEM((ts + halo, d), BF16),
            pltpu.VMEM((2, ts + halo, fc), F32),
            pltpu.VMEM((ts, nc * fc), BF16),
        ],
        compiler_params=_params(2),
        name="conv_ffn_final" if final else "conv_ffn",
    )(*args)


def _rwkv_proj_kernel(x_ref, halo_ref, gm_ref, mu_ref, wr_ref, wk_ref, wv_ref,
                      w0_ref, w1_ref, w2_ref, a0_ref, a1_ref, a2_ref, g1_ref, g2_ref,
                      r_ref, k_ref, v_ref, lw_ref, a_ref, g_ref, xs_ref):
    i = pl.program_id(1)
    ts = x_ref.shape[1]
    halo = HALO_F32
    gm = gm_ref[...]
    xn = _rms(x_ref[0], gm)
    xs_ref[0:halo, :] = _rms(halo_ref[0], gm)
    xs_ref[halo:, :] = xn

    @pl.when(i == 0)
    def _():
        xs_ref[0:halo, :] = jnp.zeros((halo, xs_ref.shape[1]), F32)

    xx = xs_ref[halo - 1:halo - 1 + ts, :] - xn

    def mix(n):
        return (xn + xx * mu_ref[n:n + 1, :]).astype(BF16)

    r_ref[0] = jnp.dot(mix(0), wr_ref[...], preferred_element_type=F32)
    k_ref[0] = jnp.dot(mix(2), wk_ref[...], preferred_element_type=F32)
    v_ref[0] = jnp.dot(mix(3), wv_ref[...], preferred_element_type=F32)

    wl = w0_ref[...] + _mm(jnp.tanh(jnp.dot(mix(1), w1_ref[...], preferred_element_type=F32)),
                           w2_ref[...])
    z = -wl
    softplus = jnp.maximum(z, 0.0) + jnp.log(1.0 + jnp.exp(-jnp.abs(z)))
    w = -softplus - 0.5
    lw_ref[0] = -jnp.exp(w)

    al = a0_ref[...] + _mm(jnp.dot(mix(4), a1_ref[...], preferred_element_type=F32), a2_ref[...])
    a_ref[0] = jax.nn.sigmoid(al)
    g_ref[0] = _mm(jax.nn.sigmoid(jnp.dot(mix(5), g1_ref[...], preferred_element_type=F32)),
                   g2_ref[...])


def _rwkv_proj_call(h, gm, mu, wr, wk, wv, w0, w1, w2, a0, a1, a2, g1, g2):
    b, s, d = h.shape
    ts = RWKV_PROJ_TILE
    halo = HALO_F32
    hpt = ts // halo
    tile = pl.BlockSpec((1, ts, d), lambda bi, i: (bi, i, 0))
    consts = [gm, mu, wr, wk, wv, w0, w1, w2, a0, a1, a2, g1, g2]
    return pl.pallas_call(
        _rwkv_proj_kernel,
        grid=(b, s // ts),
        in_specs=[tile,
                  pl.BlockSpec((1, halo, d), lambda bi, i: (bi, jnp.maximum(i * hpt - 1, 0), 0))]
                 + [_const_spec(c.shape) for c in consts],
        out_specs=[tile] * 6,
        out_shape=[jax.ShapeDtypeStruct((b, s, d), F32)] * 6,
        scratch_shapes=[pltpu.VMEM((ts + halo, d), F32)],
        compiler_params=_params(2),
        name="rwkv_proj",
    )(h, h, *consts)


def _rwkv_core_kernel(r_ref, k_ref, v_ref, lw_ref, a_ref, kk_ref, ka_ref, rk_ref,
                      lnw_ref, lnb_ref, o_ref, h_ref):
    c = RWKV_CHUNK
    hd = HEAD_DIM
    c_shift = int(math.log2(c))
    step = r_ref.shape[1]
    nch = step // c

    @pl.when(pl.program_id(2) == 0)
    def _():
        h_ref[...] = jnp.zeros_like(h_ref)

    lane = lax.broadcasted_iota(jnp.int32, (1, LANES), 1)
    head0 = lane < hd

    def by_head(x0, x1):
        return jnp.where(head0, x0, x1)

    def seg_sum(t):
        s0 = jnp.sum(jnp.where(head0, t, 0.0), axis=-1, keepdims=True)
        s1 = jnp.sum(jnp.where(head0, 0.0, t), axis=-1, keepdims=True)
        return by_head(s0, s1)

    r, k, v, lw, a = r_ref[0], k_ref[0], v_ref[0], lw_ref[0], a_ref[0]

    sub = min(step, RWKV_CUMSUM_ROWS)
    tr = lax.broadcasted_iota(jnp.int32, (sub, sub), 0)
    tc = lax.broadcasted_iota(jnp.int32, (sub, sub), 1)
    same_chunk = lax.shift_right_logical(tr, c_shift) == lax.shift_right_logical(tc, c_shift)
    ltri = jnp.where((tr >= tc) & same_chunk, 1.0, 0.0).astype(BF16)
    hi = lw.astype(BF16)
    rem = lw - hi.astype(F32)
    mid = rem.astype(BF16)
    lo = (rem - mid.astype(F32)).astype(BF16)
    split = jnp.concatenate([hi, mid, lo], axis=1)
    cum3 = jnp.concatenate(
        [jnp.dot(ltri, split[n * sub:(n + 1) * sub], preferred_element_type=F32)
         for n in range(step // sub)], axis=0)
    cum = cum3[:, :LANES] + cum3[:, LANES:2 * LANES] + cum3[:, 2 * LANES:]

    e_w = jnp.exp(cum)
    e_wi = jnp.exp(-cum)
    e_wp = jnp.exp(cum - lw)
    kk = k * kk_ref[...]
    kk = kk / jnp.maximum(jnp.sqrt(seg_sum(kk * kk)), 1e-12)
    kmod = k * (1.0 + (a - 1.0) * ka_ref[...])
    a_t = -kk * e_wp
    b_t = kk * a * e_wi
    k_t = kmod * e_wi
    r_t = r * e_w

    rows = [slice(ch * c, (ch + 1) * c) for ch in range(nch)]
    w_c = [e_w[ch * c + c - 1:ch * c + c, :] for ch in range(nch)]
    zeros_c = jnp.zeros((c, LANES), F32)

    r4 = lax.broadcasted_iota(jnp.int32, (4 * c, LANES), 0)
    l4 = lax.broadcasted_iota(jnp.int32, (4 * c, LANES), 1)
    t_idx = r4 & (c - 1)
    s_idx = l4 & (c - 1)
    incl = lax.shift_right_logical(r4, c_shift) & 1
    tri_mask = t_idx - s_idx + incl > 0
    r2 = lax.broadcasted_iota(jnp.int32, (LANES, LANES), 0)
    l2 = lax.broadcasted_iota(jnp.int32, (LANES, LANES), 1)
    eye = r2 == l2
    blockdiag = (r2 < hd) == (l2 < hd)
    eye_f = jnp.where(eye, 1.0, 0.0)

    aa, rr = [], []
    for rw in rows:
        a_c, r_c = a_t[rw], r_t[rw]
        lhs = jnp.concatenate([by_head(a_c, 0.0), by_head(r_c, 0.0),
                               by_head(0.0, a_c), by_head(0.0, r_c)], axis=0)
        rhs = jnp.concatenate([b_t[rw], k_t[rw]], axis=0)
        res = jnp.where(tri_mask, _mm_nt(lhs, rhs), 0.0)
        aa.append(jnp.concatenate([res[0:c], res[2 * c:3 * c]], axis=0))
        rr.append(jnp.concatenate([res[c:2 * c], res[3 * c:4 * c]], axis=0))

    rp, y0, gg, dd = ([None] * nch for _ in range(4))

    def parallel_stages(chunks, tick):
        rws = [rows[ch] for ch in chunks]
        wcs = [w_c[ch] for ch in chunks]
        aas = [aa[ch] for ch in chunks]
        n_bd = [jnp.where(blockdiag, by_head(x, pltpu.roll(x, hd, 1)), 0.0) for x in aas]
        inv = [eye_f + n for n in n_bd]
        p = n_bd
        for _ in range(c_shift - 1):
            p = [_mm(x, x) for x in p]
            tick()
            inv = [t + _mm(t, x) for t, x in zip(inv, p)]
            tick()
        akv = [_mm(x, jnp.concatenate([zeros_c, v[rw]], axis=0)) for x, rw in zip(aas, rws)]
        tick()
        ua = [_mm(t, jnp.concatenate([x, jnp.concatenate([a_t[rw], a_t[rw]], axis=0)], axis=1))
              for t, x, rw in zip(inv, akv, rws)]
        tick()
        u0 = [by_head(x[0:c, :LANES], x[c:2 * c, :LANES]) for x in ua]
        a2 = [by_head(x[0:c, LANES:], x[c:2 * c, LANES:]) for x in ua]
        rhs4 = [jnp.concatenate([jnp.concatenate([x, zeros_c], axis=0),
                                 jnp.concatenate([u, v[rw]], axis=0)], axis=1)
                for x, u, rw in zip(a2, u0, rws)]
        res4 = [_mm(rr[ch], y) for ch, y in zip(chunks, rhs4)]
        tick()
        gd = [_mm_tn(jnp.concatenate([b_t[rw] * w, k_t[rw] * w], axis=0), y)
              for rw, w, y in zip(rws, wcs, rhs4)]
        tick()
        for n, ch in enumerate(chunks):
            x = res4[n]
            rp[ch] = r_t[rws[n]] + by_head(x[0:c, :LANES], x[c:2 * c, :LANES])
            y0[ch] = by_head(x[0:c, LANES:], x[c:2 * c, LANES:])
            gg[ch] = jnp.where(blockdiag, gd[n][:, :LANES], 0.0) + jnp.where(eye, wcs[n], 0.0)
            dd[ch] = jnp.where(blockdiag, gd[n][:, LANES:], 0.0)

    chain = {"hh": h_ref[...], "todo": []}
    ys = [None] * nch

    def chain_step():
        if chain["todo"]:
            ch = chain["todo"].pop(0)
            ys[ch] = _mm(rp[ch], chain["hh"]) + y0[ch]
            chain["hh"] = _mm(gg[ch], chain["hh"]) + dd[ch]

    first = list(range(nch // 2))
    second = list(range(nch // 2, nch))
    parallel_stages(first, lambda: None)
    chain["todo"] += first
    parallel_stages(second, chain_step)
    chain["todo"] += second
    while chain["todo"]:
        chain_step()
    h_ref[...] = chain["hh"]
    y = jnp.concatenate(ys, axis=0)

    mean = seg_sum(y) * (1.0 / hd)
    dev = y - mean
    var = seg_sum(dev * dev) * (1.0 / hd)
    yn = dev * lax.rsqrt(var + LNX_EPS) * lnw_ref[...] + lnb_ref[...]
    bonus = seg_sum(r * kmod * rk_ref[...]) * v
    o_ref[0] = yn + bonus


def _rwkv_core_call(r, k, v, lw, a, k_k, k_a, r_k, ln_w, ln_b):
    b, s, d = r.shape
    step = RWKV_STEP
    tile = pl.BlockSpec((1, step, LANES), lambda bi, hp, i: (bi, i, hp))
    par = pl.BlockSpec((1, LANES), lambda bi, hp, i: (0, hp))
    return pl.pallas_call(
        _rwkv_core_kernel,
        grid=(b, d // LANES, s // step),
        in_specs=[tile] * 5 + [par] * 5,
        out_specs=tile,
        out_shape=jax.ShapeDtypeStruct((b, s, d), F32),
        scratch_shapes=[pltpu.VMEM((LANES, LANES), F32)],
        compiler_params=_params(3),
        name="rwkv_recurrence",
    )(r, k, v, lw, a, k_k, k_a, r_k, ln_w, ln_b)


def _rope_tables(seq):
    inv = 1.0 / (ROPE_THETA ** (jnp.arange(0, HEAD_DIM, 2, dtype=F32) / HEAD_DIM))
    ang = jnp.arange(seq, dtype=F32)[:, None] * inv[None, :]
    cos, sin = jnp.cos(ang), jnp.sin(ang)
    cos = jnp.concatenate([cos, cos] * HEADS_PER_PAIR, axis=-1)
    sin = jnp.concatenate([-sin, sin] * HEADS_PER_PAIR, axis=-1)
    return cos, sin


def _ffn_weights(w_gate, w_up, conv_w, conv_b, w_down):
    d, f = w_gate.shape
    fc = FFN_FCHUNK
    nc = f // fc
    wg = w_gate.astype(BF16).reshape(d, nc, fc).transpose(1, 0, 2)
    wu = w_up.astype(BF16).reshape(d, nc, fc).transpose(1, 0, 2)
    cw = conv_w.reshape(CONV_WIDTH, nc, fc).transpose(1, 0, 2)
    cb = conv_b.reshape(nc, 1, fc)
    wd = w_down.astype(BF16)
    return wg, wu, cw, cb, wd


def kernel(x, norm_mix, norm_ffn, norm_final, attn_w_qkv, attn_w_o, rwkv_mu, rwkv_w_rkv, rwkv_w0, rwkv_w1, rwkv_w2, rwkv_a0, rwkv_a1, rwkv_a2, rwkv_g1, rwkv_g2, rwkv_k_k, rwkv_k_a, rwkv_r_k, rwkv_lnx_w, rwkv_lnx_b, rwkv_w_o, ffn_w_gate, ffn_w_up, ffn_conv_w, ffn_conv_b, ffn_w_down):
    b, s, d = x.shape
    depth = norm_mix.shape[0]
    n_mixers = 2
    row = lambda t: t.reshape(1, d)
    cos, sin = _rope_tables(s)
    h = x
    for i in range(depth):
        j = i // n_mixers
        if i % n_mixers == 0:
            w_qkv = attn_w_qkv[j].astype(BF16)
            q, k, vt, km = _qkv_call(h, row(norm_mix[i]), w_qkv[:, :d], w_qkv[:, d:2 * d],
                                     w_qkv[:, 2 * d:].T, cos, sin)
            attn = _moba_call(q, k, vt, km)
            h = _oproj_call(h, attn, None, attn_w_o[j].astype(BF16))
        else:
            bf = lambda t: t.astype(BF16)
            r, k, v, lw, a, g = _rwkv_proj_call(
                h, row(norm_mix[i]), rwkv_mu[j], bf(rwkv_w_rkv[j, 0]), bf(rwkv_w_rkv[j, 1]),
                bf(rwkv_w_rkv[j, 2]), row(rwkv_w0[j]), bf(rwkv_w1[j]), bf(rwkv_w2[j]),
                row(rwkv_a0[j]), bf(rwkv_a1[j]), bf(rwkv_a2[j]), bf(rwkv_g1[j]), bf(rwkv_g2[j]))
            y = _rwkv_core_call(r, k, v, lw, a, row(rwkv_k_k[j]), row(rwkv_k_a[j]),
                                row(rwkv_r_k[j]), row(rwkv_lnx_w[j]), row(rwkv_lnx_b[j]))
            h = _oproj_call(h, y, g, bf(rwkv_w_o[j]))
        g_final = row(norm_final) if i == depth - 1 else None
        h = _ffn_call(h, row(norm_ffn[i]),
                      *_ffn_weights(ffn_w_gate[i], ffn_w_up[i], ffn_conv_w[i], ffn_conv_b[i],
                                    ffn_w_down[i]), g_final)
    return h
```

```python
import functools
import math

import jax
import jax.numpy as jnp
from jax import lax
from jax.experimental import pallas as pl
from jax.experimental.pallas import tpu as pltpu

F32 = jnp.float32
BF16 = jnp.bfloat16
HIGHEST = lax.Precision.HIGHEST

LANES = 128
HEAD_DIM = 64
HEADS_PER_PAIR = LANES // HEAD_DIM
MOBA_BLOCK = 256
MOBA_TOPK = 3
MOBA_GROUP = 4
MOBA_HEADS = 4
ROPE_THETA = 10000.0
RMS_EPS = 1e-6
LNX_EPS = 64e-5
CONV_WIDTH = 3
RWKV_CHUNK = 64
RWKV_STEP = 2048
RWKV_CUMSUM_ROWS = 256
RWKV_PROJ_TILE = 256
FFN_FCHUNK = 256
TOKEN_TILE = 512
HALO_F32 = 8
HALO_BF16 = 16
VMEM_LIMIT = 56 * 1024 * 1024

_NT = (((1,), (1,)), ((), ()))
_TN = (((0,), (0,)), ((), ()))


def _mm(a, b):
    return jnp.dot(a.astype(BF16), b.astype(BF16), preferred_element_type=F32)


def _mm_nt(a, b):
    return lax.dot_general(a.astype(BF16), b.astype(BF16), _NT, preferred_element_type=F32)


def _mm_tn(a, b):
    return lax.dot_general(a.astype(BF16), b.astype(BF16), _TN, preferred_element_type=F32)


def _rms(x, g):
    y = x * lax.rsqrt(jnp.mean(x * x, axis=-1, keepdims=True) + RMS_EPS)
    return y * g


def _params(n_axes):
    return pltpu.CompilerParams(
        dimension_semantics=("arbitrary",) * n_axes, vmem_limit_bytes=VMEM_LIMIT)


def _const_spec(shape):
    zeros = (0,) * len(shape)
    return pl.BlockSpec(shape, lambda *_: zeros, pipeline_mode=pl.Buffered(1))


def _qkv_kernel(x_ref, g_ref, wq_ref, wk_ref, wvt_ref, cos_ref, sin_ref,
                q_ref, k_ref, vt_ref, km_ref, *, scale):
    ts, d = x_ref.shape[1], x_ref.shape[2]
    xb = _rms(x_ref[0], g_ref[...]).astype(BF16)
    q = jnp.dot(xb, wq_ref[...], preferred_element_type=F32)
    k = jnp.dot(xb, wk_ref[...], preferred_element_type=F32)
    vt = lax.dot_general(wvt_ref[...], xb, _NT, preferred_element_type=F32)

    reps = d // LANES
    cos = jnp.concatenate([cos_ref[...]] * reps, axis=1)
    sin = jnp.concatenate([sin_ref[...]] * reps, axis=1)
    lane = lax.broadcasted_iota(jnp.int32, (1, d), 1)
    first_half = (lane & (HEAD_DIM - 1)) < HEAD_DIM // 2

    def rope(t):
        partner = jnp.where(first_half,
                            pltpu.roll(t, d - HEAD_DIM // 2, 1),
                            pltpu.roll(t, HEAD_DIM // 2, 1))
        return t * cos + partner * sin

    q = rope(q) * scale
    k = rope(k)
    q_ref[0] = q.astype(BF16)
    k_ref[0] = k.astype(BF16)
    for r in range(ts // MOBA_BLOCK):
        rows = slice(r * MOBA_BLOCK, (r + 1) * MOBA_BLOCK)
        vt_ref[0, r] = vt[:, rows].astype(BF16)
        km_ref[0, r] = jnp.mean(k[rows], axis=0, keepdims=True)


def _qkv_call(x, g, wq, wk, wvt, cos, sin):
    b, s, d = x.shape
    ts = TOKEN_TILE
    nb = s // MOBA_BLOCK
    bpt = ts // MOBA_BLOCK
    scale = math.log2(math.e) / math.sqrt(HEAD_DIM)
    return pl.pallas_call(
        functools.partial(_qkv_kernel, scale=scale),
        grid=(b, s // ts),
        in_specs=[
            pl.BlockSpec((1, ts, d), lambda bi, i: (bi, i, 0)),
            _const_spec((1, d)),
            _const_spec((d, d)),
            _const_spec((d, d)),
            _const_spec((d, d)),
            pl.BlockSpec((ts, LANES), lambda bi, i: (i, 0)),
            pl.BlockSpec((ts, LANES), lambda bi, i: (i, 0)),
        ],
        out_specs=[
            pl.BlockSpec((1, ts, d), lambda bi, i: (bi, i, 0)),
            pl.BlockSpec((1, ts, d), lambda bi, i: (bi, i, 0)),
            pl.BlockSpec((1, bpt, d, MOBA_BLOCK), lambda bi, i: (bi, i, 0, 0)),
            pl.BlockSpec((1, bpt, 1, d), lambda bi, i: (bi, i, 0, 0)),
        ],
        out_shape=[
            jax.ShapeDtypeStruct((b, s, d), BF16),
            jax.ShapeDtypeStruct((b, s, d), BF16),
            jax.ShapeDtypeStruct((b, nb, d, MOBA_BLOCK), BF16),
            jax.ShapeDtypeStruct((b, nb, 1, d), F32),
        ],
        compiler_params=_params(2),
        name="qkv_rope",
    )(x, g, wq, wk, wvt, cos, sin)


def _moba_kernel(q_ref, k_ref, vt_ref, km_ref, o_ref, bias_ref, sa_ref, sb_ref, ca_ref, cb_ref):
    i = pl.program_id(2)
    blk = MOBA_BLOCK
    nb = km_ref.shape[1]
    neg_inf = -jnp.inf
    width = q_ref.shape[2]
    heads = range(width // HEAD_DIM)
    q = q_ref[0]
    km = km_ref[0, :, 0, :]
    lane = lax.broadcasted_iota(jnp.int32, (1, width), 1)
    jrow = lax.broadcasted_iota(jnp.int32, (nb, blk), 0)
    past = jrow < i

    in_heads = [(lane >= hh * HEAD_DIM) & (lane < (hh + 1) * HEAD_DIM) for hh in heads]
    qms = [jnp.where(in_head, q, jnp.zeros_like(q)) for in_head in in_heads]

    grp = MOBA_GROUP
    assert nb % grp == 0
    last_group = nb // grp - 1
    buf_a = (sa_ref, ca_ref)
    buf_b = (sb_ref, cb_ref)

    def store_scores(dst, hh, s):
        s_ref, cmax_ref = dst
        s_ref[hh] = s
        for jj in range(grp):
            cmax_ref[hh, jj:jj + 1, :] = jnp.max(
                s[jj * blk:(jj + 1) * blk], axis=0, keepdims=True).astype(F32)

    def scores_head(dst, g, hh):
        g = jnp.minimum(g, last_group)
        k_grp = k_ref[0, pl.ds(pl.multiple_of(g * (grp * blk), grp * blk), grp * blk), :]
        store_scores(dst, hh, lax.dot_general(k_grp, qms[hh], _NT,
                                              preferred_element_type=F32).astype(BF16))

    km_hi = km.astype(BF16)
    km_rem = km - km_hi.astype(F32)
    km_mid = km_rem.astype(BF16)
    km_lo = (km_rem - km_mid.astype(F32)).astype(BF16)
    lhs0 = jnp.concatenate([k_ref[0, 0:grp * blk, :], km_hi, km_mid, km_lo], axis=0)
    n0 = grp * blk

    for hh in heads:
        r0 = lax.dot_general(lhs0, qms[hh], _NT, preferred_element_type=F32)
        store_scores(buf_a, hh, r0[:n0].astype(BF16))
        gate = r0[n0:n0 + nb] + r0[n0 + nb:n0 + 2 * nb] + r0[n0 + 2 * nb:]
        g = jnp.where(past, gate, neg_inf)
        sel = jnp.zeros((nb, blk), jnp.bool_)
        for _ in range(MOBA_TOPK):
            m = jnp.max(g, axis=0, keepdims=True)
            idx = jnp.min(jnp.where(g == m, jrow, nb), axis=0, keepdims=True)
            pick = jrow == idx
            sel = sel | pick
            g = jnp.where(pick, neg_inf, g)
        bias = jnp.where(sel & past, 0.0, neg_inf)
        for gg in range(nb // MOBA_GROUP):
            bias_ref[hh, gg] = bias[gg * MOBA_GROUP:(gg + 1) * MOBA_GROUP]

    ones_rows = jnp.ones((HALO_BF16, blk), BF16)

    def v_aug(j, hh):
        vt = vt_ref[0, j, hh * HEAD_DIM:(hh + 1) * HEAD_DIM, :]
        return jnp.concatenate([vt, ones_rows], axis=0)

    krow = lax.broadcasted_iota(jnp.int32, (blk, blk), 0)
    qcol = lax.broadcasted_iota(jnp.int32, (blk, blk), 1)
    causal = krow <= qcol
    k_own = k_ref[0, pl.ds(pl.multiple_of(i * blk, blk), blk), :]
    state = []
    own_scores = [lax.dot_general(k_own, qm, _NT, preferred_element_type=F32) for qm in qms]
    for hh in heads:
        s = jnp.where(causal, own_scores[hh], neg_inf)
        m = jnp.max(s, axis=0, keepdims=True).astype(BF16).astype(F32)
        p = jnp.exp2((s - m).astype(BF16))
        acc = jnp.dot(v_aug(i, hh), p, preferred_element_type=F32)
        state += [m, acc]

    def consume_head(src, g, hh, m, acc):
        s_ref, cmax_ref = src
        j0 = g * grp
        bias = bias_ref[hh, g]
        m_blk = jnp.max(cmax_ref[hh] + bias, axis=0, keepdims=True)
        m_new = jnp.maximum(m, m_blk)
        alpha = jnp.exp2(m - m_new)
        shift = (bias - m_new).astype(BF16)
        p = jnp.concatenate(
            [jnp.exp2(s_ref[hh, jj * blk:(jj + 1) * blk, :] + shift[jj:jj + 1])
             for jj in range(grp)], axis=0)
        vt = jnp.concatenate([v_aug(j0 + jj, hh) for jj in range(grp)], axis=1)
        return m_new, alpha * acc + jnp.dot(vt, p, preferred_element_type=F32)

    def consume(src, g, carry):
        out = []
        for hh in heads:
            out += consume_head(src, g, hh, *carry[2 * hh:2 * hh + 2])
        return tuple(out)

    def overlapped(dst, g_next, src, g, carry):
        out = []
        for hh in heads:
            scores_head(dst, g_next, hh)
            out += consume_head(src, g, hh, *carry[2 * hh:2 * hh + 2])
        return tuple(out)

    n_groups = lax.shift_right_logical(i + (grp - 1), int(math.log2(grp)))
    n_pairs = lax.shift_right_logical(n_groups, 1)

    def body(t, carry):
        g0 = 2 * t
        carry = overlapped(buf_b, g0 + 1, buf_a, g0, carry)
        return overlapped(buf_a, g0 + 2, buf_b, g0 + 1, carry)

    state = lax.fori_loop(0, n_pairs, body, tuple(state))
    state = lax.cond((n_groups & 1) == 1,
                     lambda c: consume(buf_a, 2 * n_pairs, c), lambda c: c, state)
    outs = []
    for hh in heads:
        acc = state[2 * hh + 1]
        outs.append(acc[:HEAD_DIM] / acc[HEAD_DIM:HEAD_DIM + 1])
    o_ref[0] = jnp.concatenate(outs, axis=0).T.astype(o_ref.dtype)


def _moba_call(q, k, vt, km):
    b, s, d = q.shape
    nb = s // MOBA_BLOCK
    width = MOBA_HEADS * HEAD_DIM
    return pl.pallas_call(
        _moba_kernel,
        grid=(b, d // width, nb),
        in_specs=[
            pl.BlockSpec((1, MOBA_BLOCK, width), lambda bi, hg, i: (bi, i, hg)),
            pl.BlockSpec((1, s, width), lambda bi, hg, i: (bi, 0, hg)),
            pl.BlockSpec((1, nb, width, MOBA_BLOCK), lambda bi, hg, i: (bi, 0, hg, 0)),
            pl.BlockSpec((1, nb, 1, width), lambda bi, hg, i: (bi, 0, 0, hg)),
        ],
        out_specs=pl.BlockSpec((1, MOBA_BLOCK, width), lambda bi, hg, i: (bi, i, hg)),
        out_shape=jax.ShapeDtypeStruct((b, s, d), BF16),
        scratch_shapes=[pltpu.VMEM((MOBA_HEADS, nb // MOBA_GROUP, MOBA_GROUP, MOBA_BLOCK), F32)]
                       + [pltpu.VMEM((MOBA_HEADS, MOBA_GROUP * MOBA_BLOCK, MOBA_BLOCK), BF16)] * 2
                       + [pltpu.VMEM((MOBA_HEADS, MOBA_GROUP, MOBA_BLOCK), F32)] * 2,
        compiler_params=_params(3),
        name="moba_attention",
    )(q, k, vt, km)


def _ffn_kernel(*refs, final, gated):
    refs = list(refs)
    x_ref, xh_ref, y_ref, yh_ref = refs[:4]
    del refs[:4]
    if gated:
        yg_ref, ygh_ref = refs[:2]
        del refs[:2]
    wo_ref, g_ref, wg_ref, wu_ref, cw_ref, cb_ref, wd_ref = refs[:7]
    del refs[:7]
    if final:
        gf_ref = refs.pop(0)
    o_ref, hn_ref, gx_ref, act_ref = refs
    i = pl.program_id(1)
    ts = x_ref.shape[1]
    nc, _, fc = wg_ref.shape
    halo = HALO_BF16

    def mixed(h_blk, y_blk, gate_blk):
        y = y_blk[0].astype(F32) * gate_blk[0] if gated else y_blk[0]
        return h_blk[0] + jnp.dot(y.astype(BF16), wo_ref[...], preferred_element_type=F32)

    x = mixed(x_ref, y_ref, yg_ref if gated else None)
    g = g_ref[...]
    hn_ref[0:halo, :] = _rms(mixed(xh_ref, yh_ref, ygh_ref if gated else None), g).astype(BF16)
    hn_ref[halo:, :] = _rms(x, g).astype(BF16)
    halo_keep = jnp.where(i == 0, 0.0, 1.0)

    for c in range(nc):
        gx = gx_ref.at[c % 2]
        gate = jnp.dot(hn_ref[...], wg_ref[c], preferred_element_type=F32)
        gx[0:halo, :] = gate[0:halo] * halo_keep
        gx[halo:, :] = gate[halo:]
        up = jnp.dot(hn_ref[halo:, :], wu_ref[c], preferred_element_type=F32)
        cw = cw_ref[c]
        conv = (gx[halo - 2:halo - 2 + ts, :] * cw[0:1]
                + gx[halo - 1:halo - 1 + ts, :] * cw[1:2]
                + gate[halo:] * cw[2:3]
                + cb_ref[c])
        act_ref[:, c * fc:(c + 1) * fc] = (conv * jax.nn.sigmoid(conv) * up).astype(BF16)

    out = x + jnp.dot(act_ref[...], wd_ref[...], preferred_element_type=F32)
    if final:
        out = _rms(out, gf_ref[...])
    o_ref[0] = out


def _ffn_call(h, y, y_gate, w_o, g, wg, wu, cw, cb, wd, g_final):
    b, s, d = h.shape
    ts = TOKEN_TILE
    nc, _, fc = wg.shape
    halo = HALO_BF16
    hpt = ts // halo
    final = g_final is not None
    gated = y_gate is not None
    tile = pl.BlockSpec((1, ts, d), lambda bi, i: (bi, i, 0))
    halo_blk = pl.BlockSpec((1, halo, d), lambda bi, i: (bi, jnp.maximum(i * hpt - 1, 0), 0))
    streams = [h, y] + ([y_gate] if gated else [])
    in_specs = [tile, halo_blk] * len(streams) + [
        _const_spec((d, d)),
        _const_spec((1, d)),
        _const_spec((nc, d, fc)),
        _const_spec((nc, d, fc)),
        _const_spec((nc, CONV_WIDTH, fc)),
        _const_spec((nc, 1, fc)),
        _const_spec((nc * fc, d)),
    ]
    args = [t for t in streams for _ in range(2)] + [w_o, g, wg, wu, cw, cb, wd]
    if final:
        in_specs.append(_const_spec((1, d)))
        args.append(g_final)
    return pl.pallas_call(
        functools.partial(_ffn_kernel, final=final, gated=gated),
        grid=(b, s // ts),
        in_specs=in_specs,
        out_specs=tile,
        out_shape=jax.ShapeDtypeStruct((b, s, d), F32),
        scratch_shapes=[
            pltpu.VMEM((ts + halo, d), BF16),
            pltpu.VMEM((2, ts + halo, fc), F32),
            pltpu.VMEM((ts, nc * fc), BF16),
        ],
        compiler_params=_params(2),
        name="conv_ffn_final" if final else "conv_ffn",
    )(*args)


def _rwkv_proj_kernel(x_ref, halo_ref, gm_ref, mu_ref, wr_ref, wk_ref, wv_ref,
                      w0_ref, w1_ref, w2_ref, a0_ref, a1_ref, a2_ref, g1_ref, g2_ref,
                      r_ref, k_ref, v_ref, lw_ref, a_ref, g_ref, xs_ref):
    i = pl.program_id(1)
    ts = x_ref.shape[1]
    halo = HALO_F32
    gm = gm_ref[...]
    xn = _rms(x_ref[0], gm)
    xs_ref[0:halo, :] = _rms(halo_ref[0], gm)
    xs_ref[halo:, :] = xn

    @pl.when(i == 0)
    def _():
        xs_ref[0:halo, :] = jnp.zeros((halo, xs_ref.shape[1]), F32)

    xx = xs_ref[halo - 1:halo - 1 + ts, :] - xn

    def mix(n):
        return (xn + xx * mu_ref[n:n + 1, :]).astype(BF16)

    r_ref[0] = jnp.dot(mix(0), wr_ref[...], preferred_element_type=F32)
    k_ref[0] = jnp.dot(mix(2), wk_ref[...], preferred_element_type=F32)
    v_ref[0] = jnp.dot(mix(3), wv_ref[...], preferred_element_type=F32)

    wl = w0_ref[...] + _mm(jnp.tanh(jnp.dot(mix(1), w1_ref[...], preferred_element_type=F32)),
                           w2_ref[...])
    z = -wl
    softplus = jnp.maximum(z, 0.0) + jnp.log(1.0 + jnp.exp(-jnp.abs(z)))
    w = -softplus - 0.5
    lw_ref[0] = -jnp.exp(w)

    al = a0_ref[...] + _mm(jnp.dot(mix(4), a1_ref[...], preferred_element_type=F32), a2_ref[...])
    a_ref[0] = jax.nn.sigmoid(al)
    g_ref[0] = _mm(jax.nn.sigmoid(jnp.dot(mix(5), g1_ref[...], preferred_element_type=F32)),
                   g2_ref[...])


def _rwkv_proj_call(h, gm, mu, wr, wk, wv, w0, w1, w2, a0, a1, a2, g1, g2):
    b, s, d = h.shape
    ts = RWKV_PROJ_TILE
    halo = HALO_F32
    hpt = ts // halo
    tile = pl.BlockSpec((1, ts, d), lambda bi, i: (bi, i, 0))
    consts = [gm, mu, wr, wk, wv, w0, w1, w2, a0, a1, a2, g1, g2]
    return pl.pallas_call(
        _rwkv_proj_kernel,
        grid=(b, s // ts),
        in_specs=[tile,
                  pl.BlockSpec((1, halo, d), lambda bi, i: (bi, jnp.maximum(i * hpt - 1, 0), 0))]
                 + [_const_spec(c.shape) for c in consts],
        out_specs=[tile] * 6,
        out_shape=[jax.ShapeDtypeStruct((b, s, d), F32)] * 6,
        scratch_shapes=[pltpu.VMEM((ts + halo, d), F32)],
        compiler_params=_params(2),
        name="rwkv_proj",
    )(h, h, *consts)


def _rwkv_core_kernel(r_ref, k_ref, v_ref, lw_ref, a_ref, kk_ref, ka_ref, rk_ref,
                      lnw_ref, lnb_ref, o_ref, h_ref):
    c = RWKV_CHUNK
    hd = HEAD_DIM
    c_shift = int(math.log2(c))
    step = r_ref.shape[1]
    nch = step // c

    @pl.when(pl.program_id(2) == 0)
    def _():
        h_ref[...] = jnp.zeros_like(h_ref)

    lane = lax.broadcasted_iota(jnp.int32, (1, LANES), 1)
    head0 = lane < hd

    def by_head(x0, x1):
        return jnp.where(head0, x0, x1)

    def seg_sum(t):
        s0 = jnp.sum(jnp.where(head0, t, 0.0), axis=-1, keepdims=True)
        s1 = jnp.sum(jnp.where(head0, 0.0, t), axis=-1, keepdims=True)
        return by_head(s0, s1)

    r, k, v, lw, a = r_ref[0], k_ref[0], v_ref[0], lw_ref[0], a_ref[0]

    sub = min(step, RWKV_CUMSUM_ROWS)
    tr = lax.broadcasted_iota(jnp.int32, (sub, sub), 0)
    tc = lax.broadcasted_iota(jnp.int32, (sub, sub), 1)
    same_chunk = lax.shift_right_logical(tr, c_shift) == lax.shift_right_logical(tc, c_shift)
    ltri = jnp.where((tr >= tc) & same_chunk, 1.0, 0.0).astype(BF16)
    hi = lw.astype(BF16)
    rem = lw - hi.astype(F32)
    mid = rem.astype(BF16)
    lo = (rem - mid.astype(F32)).astype(BF16)
    split = jnp.concatenate([hi, mid, lo], axis=1)
    cum3 = jnp.concatenate(
        [jnp.dot(ltri, split[n * sub:(n + 1) * sub], preferred_element_type=F32)
         for n in range(step // sub)], axis=0)
    cum = cum3[:, :LANES] + cum3[:, LANES:2 * LANES] + cum3[:, 2 * LANES:]

    e_w = jnp.exp(cum)
    e_wi = jnp.exp(-cum)
    e_wp = jnp.exp(cum - lw)
    kk = k * kk_ref[...]
    kk = kk / jnp.maximum(jnp.sqrt(seg_sum(kk * kk)), 1e-12)
    kmod = k * (1.0 + (a - 1.0) * ka_ref[...])
    a_t = -kk * e_wp
    b_t = kk * a * e_wi
    k_t = kmod * e_wi
    r_t = r * e_w

    rows = [slice(ch * c, (ch + 1) * c) for ch in range(nch)]
    w_c = [e_w[ch * c + c - 1:ch * c + c, :] for ch in range(nch)]
    zeros_c = jnp.zeros((c, LANES), F32)

    r4 = lax.broadcasted_iota(jnp.int32, (4 * c, LANES), 0)
    l4 = lax.broadcasted_iota(jnp.int32, (4 * c, LANES), 1)
    t_idx = r4 & (c - 1)
    s_idx = l4 & (c - 1)
    incl = lax.shift_right_logical(r4, c_shift) & 1
    tri_mask = t_idx - s_idx + incl > 0
    r2 = lax.broadcasted_iota(jnp.int32, (LANES, LANES), 0)
    l2 = lax.broadcasted_iota(jnp.int32, (LANES, LANES), 1)
    eye = r2 == l2
    blockdiag = (r2 < hd) == (l2 < hd)
    eye_f = jnp.where(eye, 1.0, 0.0)

    aa, rr = [], []
    for rw in rows:
        a_c, r_c = a_t[rw], r_t[rw]
        lhs = jnp.concatenate([by_head(a_c, 0.0), by_head(r_c, 0.0),
                               by_head(0.0, a_c), by_head(0.0, r_c)], axis=0)
        rhs = jnp.concatenate([b_t[rw], k_t[rw]], axis=0)
        res = jnp.where(tri_mask, _mm_nt(lhs, rhs), 0.0)
        aa.append(jnp.concatenate([res[0:c], res[2 * c:3 * c]], axis=0))
        rr.append(jnp.concatenate([res[c:2 * c], res[3 * c:4 * c]], axis=0))

    rp, y0, gg, dd = ([None] * nch for _ in range(4))

    def parallel_stages(chunks, tick):
        rws = [rows[ch] for ch in chunks]
        wcs = [w_c[ch] for ch in chunks]
        aas = [aa[ch] for ch in chunks]
        n_bd = [jnp.where(blockdiag, by_head(x, pltpu.roll(x, hd, 1)), 0.0) for x in aas]
        inv = [eye_f + n for n in n_bd]
        p = n_bd
        for _ in range(c_shift - 1):
            p = [_mm(x, x) for x in p]
            tick()
            inv = [t + _mm(t, x) for t, x in zip(inv, p)]
            tick()
        akv = [_mm(x, jnp.concatenate([zeros_c, v[rw]], axis=0)) for x, rw in zip(aas, rws)]
        tick()
        ua = [_mm(t, jnp.concatenate([x, jnp.concatenate([a_t[rw], a_t[rw]], axis=0)], axis=1))
              for t, x, rw in zip(inv, akv, rws)]
        tick()
        u0 = [by_head(x[0:c, :LANES], x[c:2 * c, :LANES]) for x in ua]
        a2 = [by_head(x[0:c, LANES:], x[c:2 * c, LANES:]) for x in ua]
        rhs4 = [jnp.concatenate([jnp.concatenate([x, zeros_c], axis=0),
                                 jnp.concatenate([u, v[rw]], axis=0)], axis=1)
                for x, u, rw in zip(a2, u0, rws)]
        res4 = [_mm(rr[ch], y) for ch, y in zip(chunks, rhs4)]
        tick()
        gd = [_mm_tn(jnp.concatenate([b_t[rw] * w, k_t[rw] * w], axis=0), y)
              for rw, w, y in zip(rws, wcs, rhs4)]
        tick()
        for n, ch in enumerate(chunks):
            x = res4[n]
            rp[ch] = r_t[rws[n]] + by_head(x[0:c, :LANES], x[c:2 * c, :LANES])
            y0[ch] = by_head(x[0:c, LANES:], x[c:2 * c, LANES:])
            gg[ch] = jnp.where(blockdiag, gd[n][:, :LANES], 0.0) + jnp.where(eye, wcs[n], 0.0)
            dd[ch] = jnp.where(blockdiag, gd[n][:, LANES:], 0.0)

    chain = {"hh": h_ref[...], "todo": []}
    ys = [None] * nch

    def chain_step():
        if chain["todo"]:
            ch = chain["todo"].pop(0)
            ys[ch] = _mm(rp[ch], chain["hh"]) + y0[ch]
            chain["hh"] = _mm(gg[ch], chain["hh"]) + dd[ch]

    first = list(range(nch // 2))
    second = list(range(nch // 2, nch))
    parallel_stages(first, lambda: None)
    chain["todo"] += first
    parallel_stages(second, chain_step)
    chain["todo"] += second
    while chain["todo"]:
        chain_step()
    h_ref[...] = chain["hh"]
    y = jnp.concatenate(ys, axis=0)

    mean = seg_sum(y) * (1.0 / hd)
    dev = y - mean
    var = seg_sum(dev * dev) * (1.0 / hd)
    yn = dev * lax.rsqrt(var + LNX_EPS) * lnw_ref[...] + lnb_ref[...]
    bonus = seg_sum(r * kmod * rk_ref[...]) * v
    o_ref[0] = yn + bonus


def _rwkv_core_call(r, k, v, lw, a, k_k, k_a, r_k, ln_w, ln_b):
    b, s, d = r.shape
    step = RWKV_STEP
    tile = pl.BlockSpec((1, step, LANES), lambda bi, hp, i: (bi, i, hp))
    par = pl.BlockSpec((1, LANES), lambda bi, hp, i: (0, hp))
    return pl.pallas_call(
        _rwkv_core_kernel,
        grid=(b, d // LANES, s // step),
        in_specs=[tile] * 5 + [par] * 5,
        out_specs=tile,
        out_shape=jax.ShapeDtypeStruct((b, s, d), F32),
        scratch_shapes=[pltpu.VMEM((LANES, LANES), F32)],
        compiler_params=_params(3),
        name="rwkv_recurrence",
    )(r, k, v, lw, a, k_k, k_a, r_k, ln_w, ln_b)


def _rope_tables(seq):
    inv = 1.0 / (ROPE_THETA ** (jnp.arange(0, HEAD_DIM, 2, dtype=F32) / HEAD_DIM))
    ang = jnp.arange(seq, dtype=F32)[:, None] * inv[None, :]
    cos, sin = jnp.cos(ang), jnp.sin(ang)
    cos = jnp.concatenate([cos, cos] * HEADS_PER_PAIR, axis=-1)
    sin = jnp.concatenate([-sin, sin] * HEADS_PER_PAIR, axis=-1)
    return cos, sin


def _ffn_weights(w_gate, w_up, conv_w, conv_b, w_down):
    d, f = w_gate.shape
    fc = FFN_FCHUNK
    nc = f // fc
    wg = w_gate.astype(BF16).reshape(d, nc, fc).transpose(1, 0, 2)
    wu = w_up.astype(BF16).reshape(d, nc, fc).transpose(1, 0, 2)
    cw = conv_w.reshape(CONV_WIDTH, nc, fc).transpose(1, 0, 2)
    cb = conv_b.reshape(nc, 1, fc)
    wd = w_down.astype(BF16)
    return wg, wu, cw, cb, wd


def kernel(x, norm_mix, norm_ffn, norm_final, attn_w_qkv, attn_w_o, rwkv_mu, rwkv_w_rkv, rwkv_w0, rwkv_w1, rwkv_w2, rwkv_a0, rwkv_a1, rwkv_a2, rwkv_g1, rwkv_g2, rwkv_k_k, rwkv_k_a, rwkv_r_k, rwkv_lnx_w, rwkv_lnx_b, rwkv_w_o, ffn_w_gate, ffn_w_up, ffn_conv_w, ffn_conv_b, ffn_w_down):
    b, s, d = x.shape
    depth = norm_mix.shape[0]
    n_mixers = 2
    row = lambda t: t.reshape(1, d)
    cos, sin = _rope_tables(s)
    h = x
    for i in range(depth):
        j = i // n_mixers
        if i % n_mixers == 0:
            w_qkv = attn_w_qkv[j].astype(BF16)
            q, k, vt, km = _qkv_call(h, row(norm_mix[i]), w_qkv[:, :d], w_qkv[:, d:2 * d],
                                     w_qkv[:, 2 * d:].T, cos, sin)
            y, y_gate, w_o = _moba_call(q, k, vt, km), None, attn_w_o[j].astype(BF16)
        else:
            bf = lambda t: t.astype(BF16)
            r, k, v, lw, a, y_gate = _rwkv_proj_call(
                h, row(norm_mix[i]), rwkv_mu[j], bf(rwkv_w_rkv[j, 0]), bf(rwkv_w_rkv[j, 1]),
                bf(rwkv_w_rkv[j, 2]), row(rwkv_w0[j]), bf(rwkv_w1[j]), bf(rwkv_w2[j]),
                row(rwkv_a0[j]), bf(rwkv_a1[j]), bf(rwkv_a2[j]), bf(rwkv_g1[j]), bf(rwkv_g2[j]))
            y = _rwkv_core_call(r, k, v, lw, a, row(rwkv_k_k[j]), row(rwkv_k_a[j]),
                                row(rwkv_r_k[j]), row(rwkv_lnx_w[j]), row(rwkv_lnx_b[j]))
            w_o = bf(rwkv_w_o[j])
        g_final = row(norm_final) if i == depth - 1 else None
        h = _ffn_call(h, y, y_gate, w_o, row(norm_ffn[i]),
                      *_ffn_weights(ffn_w_gate[i], ffn_w_up[i], ffn_conv_w[i], ffn_conv_b[i],
                                    ffn_w_down[i]), g_final)
    return h
```

```python
import functools
import math

import jax
import jax.numpy as jnp
from jax import lax
from jax.experimental import pallas as pl
from jax.experimental.pallas import tpu as pltpu

F32 = jnp.float32
BF16 = jnp.bfloat16
HIGHEST = lax.Precision.HIGHEST

LANES = 128
HEAD_DIM = 64
HEADS_PER_PAIR = LANES // HEAD_DIM
MOBA_BLOCK = 256
MOBA_TOPK = 3
MOBA_GROUP = 4
MOBA_HEADS = 4
ROPE_THETA = 10000.0
RMS_EPS = 1e-6
LNX_EPS = 64e-5
CONV_WIDTH = 3
RWKV_CHUNK = 64
RWKV_STEP = 4096
RWKV_PARTS = 4
RWKV_CUMSUM_ROWS = 256
RWKV_PROJ_TILE = 512
FFN_FCHUNK = 256
TOKEN_TILE = 512
HALO_F32 = 8
HALO_BF16 = 16
VMEM_LIMIT = 56 * 1024 * 1024

_NT = (((1,), (1,)), ((), ()))
_TN = (((0,), (0,)), ((), ()))


def _mm(a, b):
    return jnp.dot(a.astype(BF16), b.astype(BF16), preferred_element_type=F32)


def _mm_nt(a, b):
    return lax.dot_general(a.astype(BF16), b.astype(BF16), _NT, preferred_element_type=F32)


def _mm_tn(a, b):
    return lax.dot_general(a.astype(BF16), b.astype(BF16), _TN, preferred_element_type=F32)


def _rms(x, g):
    y = x * lax.rsqrt(jnp.mean(x * x, axis=-1, keepdims=True) + RMS_EPS)
    return y * g


def _params(n_axes):
    return pltpu.CompilerParams(
        dimension_semantics=("arbitrary",) * n_axes, vmem_limit_bytes=VMEM_LIMIT)


def _const_spec(shape):
    zeros = (0,) * len(shape)
    return pl.BlockSpec(shape, lambda *_: zeros, pipeline_mode=pl.Buffered(1))


def _qkv_kernel(x_ref, g_ref, wq_ref, wk_ref, wvt_ref, cos_ref, sin_ref,
                q_ref, k_ref, vt_ref, km_ref, *, scale):
    ts, d = x_ref.shape[1], x_ref.shape[2]
    xb = _rms(x_ref[0], g_ref[...]).astype(BF16)
    q = jnp.dot(xb, wq_ref[...], preferred_element_type=F32)
    k = jnp.dot(xb, wk_ref[...], preferred_element_type=F32)
    vt = lax.dot_general(wvt_ref[...], xb, _NT, preferred_element_type=F32)

    reps = d // LANES
    cos = jnp.concatenate([cos_ref[...]] * reps, axis=1)
    sin = jnp.concatenate([sin_ref[...]] * reps, axis=1)
    lane = lax.broadcasted_iota(jnp.int32, (1, d), 1)
    first_half = (lane & (HEAD_DIM - 1)) < HEAD_DIM // 2

    def rope(t):
        partner = jnp.where(first_half,
                            pltpu.roll(t, d - HEAD_DIM // 2, 1),
                            pltpu.roll(t, HEAD_DIM // 2, 1))
        return t * cos + partner * sin

    q = rope(q) * scale
    k = rope(k)
    q_ref[0] = q.astype(BF16)
    k_ref[0] = k.astype(BF16)
    for r in range(ts // MOBA_BLOCK):
        rows = slice(r * MOBA_BLOCK, (r + 1) * MOBA_BLOCK)
        vt_ref[0, r] = vt[:, rows].astype(BF16)
        km_ref[0, r] = jnp.mean(k[rows], axis=0, keepdims=True)


def _qkv_call(x, g, wq, wk, wvt, cos, sin):
    b, s, d = x.shape
    ts = TOKEN_TILE
    nb = s // MOBA_BLOCK
    bpt = ts // MOBA_BLOCK
    scale = math.log2(math.e) / math.sqrt(HEAD_DIM)
    return pl.pallas_call(
        functools.partial(_qkv_kernel, scale=scale),
        grid=(b, s // ts),
        in_specs=[
            pl.BlockSpec((1, ts, d), lambda bi, i: (bi, i, 0)),
            _const_spec((1, d)),
            _const_spec((d, d)),
            _const_spec((d, d)),
            _const_spec((d, d)),
            pl.BlockSpec((ts, LANES), lambda bi, i: (i, 0)),
            pl.BlockSpec((ts, LANES), lambda bi, i: (i, 0)),
        ],
        out_specs=[
            pl.BlockSpec((1, ts, d), lambda bi, i: (bi, i, 0)),
            pl.BlockSpec((1, ts, d), lambda bi, i: (bi, i, 0)),
            pl.BlockSpec((1, bpt, d, MOBA_BLOCK), lambda bi, i: (bi, i, 0, 0)),
            pl.BlockSpec((1, bpt, 1, d), lambda bi, i: (bi, i, 0, 0)),
        ],
        out_shape=[
            jax.ShapeDtypeStruct((b, s, d), BF16),
            jax.ShapeDtypeStruct((b, s, d), BF16),
            jax.ShapeDtypeStruct((b, nb, d, MOBA_BLOCK), BF16),
            jax.ShapeDtypeStruct((b, nb, 1, d), F32),
        ],
        compiler_params=_params(2),
        name="qkv_rope",
    )(x, g, wq, wk, wvt, cos, sin)


def _moba_kernel(q_ref, k_ref, vt_ref, km_ref, o_ref, bias_ref, sa_ref, sb_ref, ca_ref, cb_ref):
    i = pl.program_id(2)
    blk = MOBA_BLOCK
    nb = km_ref.shape[1]
    neg_inf = -jnp.inf
    width = q_ref.shape[2]
    heads = range(width // HEAD_DIM)
    q = q_ref[0]
    km = km_ref[0, :, 0, :]
    lane = lax.broadcasted_iota(jnp.int32, (1, width), 1)
    jrow = lax.broadcasted_iota(jnp.int32, (nb, blk), 0)
    past = jrow < i

    in_heads = [(lane >= hh * HEAD_DIM) & (lane < (hh + 1) * HEAD_DIM) for hh in heads]
    qms = [jnp.where(in_head, q, jnp.zeros_like(q)) for in_head in in_heads]

    grp = MOBA_GROUP
    assert nb % grp == 0
    last_group = nb // grp - 1
    buf_a = (sa_ref, ca_ref)
    buf_b = (sb_ref, cb_ref)

    def store_scores(dst, hh, s):
        s_ref, cmax_ref = dst
        s_ref[hh] = s
        for jj in range(grp):
            cmax_ref[hh, jj:jj + 1, :] = jnp.max(
                s[jj * blk:(jj + 1) * blk], axis=0, keepdims=True).astype(F32)

    def scores_head(dst, g, hh):
        g = jnp.minimum(g, last_group)
        k_grp = k_ref[0, pl.ds(pl.multiple_of(g * (grp * blk), grp * blk), grp * blk), :]
        store_scores(dst, hh, lax.dot_general(k_grp, qms[hh], _NT,
                                              preferred_element_type=F32).astype(BF16))

    km_hi = km.astype(BF16)
    km_rem = km - km_hi.astype(F32)
    km_mid = km_rem.astype(BF16)
    km_lo = (km_rem - km_mid.astype(F32)).astype(BF16)
    lhs0 = jnp.concatenate([k_ref[0, 0:grp * blk, :], km_hi, km_mid, km_lo], axis=0)
    n0 = grp * blk

    for hh in heads:
        r0 = lax.dot_general(lhs0, qms[hh], _NT, preferred_element_type=F32)
        store_scores(buf_a, hh, r0[:n0].astype(BF16))
        gate = r0[n0:n0 + nb] + r0[n0 + nb:n0 + 2 * nb] + r0[n0 + 2 * nb:]
        g = jnp.where(past, gate, neg_inf)
        sel = jnp.zeros((nb, blk), jnp.bool_)
        for _ in range(MOBA_TOPK):
            m = jnp.max(g, axis=0, keepdims=True)
            idx = jnp.min(jnp.where(g == m, jrow, nb), axis=0, keepdims=True)
            pick = jrow == idx
            sel = sel | pick
            g = jnp.where(pick, neg_inf, g)
        bias = jnp.where(sel & past, 0.0, neg_inf)
        for gg in range(nb // MOBA_GROUP):
            bias_ref[hh, gg] = bias[gg * MOBA_GROUP:(gg + 1) * MOBA_GROUP]

    ones_rows = jnp.ones((HALO_BF16, blk), BF16)

    def v_aug(j, hh):
        vt = vt_ref[0, j, hh * HEAD_DIM:(hh + 1) * HEAD_DIM, :]
        return jnp.concatenate([vt, ones_rows], axis=0)

    krow = lax.broadcasted_iota(jnp.int32, (blk, blk), 0)
    qcol = lax.broadcasted_iota(jnp.int32, (blk, blk), 1)
    causal = krow <= qcol
    k_own = k_ref[0, pl.ds(pl.multiple_of(i * blk, blk), blk), :]
    state = []
    own_scores = [lax.dot_general(k_own, qm, _NT, preferred_element_type=F32) for qm in qms]
    for hh in heads:
        s = jnp.where(causal, own_scores[hh], neg_inf)
        m = jnp.max(s, axis=0, keepdims=True).astype(BF16).astype(F32)
        p = jnp.exp2((s - m).astype(BF16))
        acc = jnp.dot(v_aug(i, hh), p, preferred_element_type=F32)
        state += [m, acc]

    def consume_head(src, g, hh, m, acc):
        s_ref, cmax_ref = src
        j0 = g * grp
        bias = bias_ref[hh, g]
        m_blk = jnp.max(cmax_ref[hh] + bias, axis=0, keepdims=True)
        m_new = jnp.maximum(m, m_blk)
        alpha = jnp.exp2(m - m_new)
        shift = (bias - m_new).astype(BF16)
        p = jnp.concatenate(
            [jnp.exp2(s_ref[hh, jj * blk:(jj + 1) * blk, :] + shift[jj:jj + 1])
             for jj in range(grp)], axis=0)
        vt = jnp.concatenate([v_aug(j0 + jj, hh) for jj in range(grp)], axis=1)
        return m_new, alpha * acc + jnp.dot(vt, p, preferred_element_type=F32)

    def consume(src, g, carry):
        out = []
        for hh in heads:
            out += consume_head(src, g, hh, *carry[2 * hh:2 * hh + 2])
        return tuple(out)

    def overlapped(dst, g_next, src, g, carry):
        out = []
        for hh in heads:
            scores_head(dst, g_next, hh)
            out += consume_head(src, g, hh, *carry[2 * hh:2 * hh + 2])
        return tuple(out)

    n_groups = lax.shift_right_logical(i + (grp - 1), int(math.log2(grp)))
    n_pairs = lax.shift_right_logical(n_groups, 1)

    def body(t, carry):
        g0 = 2 * t
        carry = overlapped(buf_b, g0 + 1, buf_a, g0, carry)
        return overlapped(buf_a, g0 + 2, buf_b, g0 + 1, carry)

    state = lax.fori_loop(0, n_pairs, body, tuple(state))
    state = lax.cond((n_groups & 1) == 1,
                     lambda c: consume(buf_a, 2 * n_pairs, c), lambda c: c, state)
    outs = []
    for hh in heads:
        acc = state[2 * hh + 1]
        outs.append(acc[:HEAD_DIM] / acc[HEAD_DIM:HEAD_DIM + 1])
    o_ref[0] = jnp.concatenate(outs, axis=0).T.astype(o_ref.dtype)


def _moba_call(q, k, vt, km):
    b, s, d = q.shape
    nb = s // MOBA_BLOCK
    width = MOBA_HEADS * HEAD_DIM
    return pl.pallas_call(
        _moba_kernel,
        grid=(b, d // width, nb),
        in_specs=[
            pl.BlockSpec((1, MOBA_BLOCK, width), lambda bi, hg, i: (bi, i, hg)),
            pl.BlockSpec((1, s, width), lambda bi, hg, i: (bi, 0, hg)),
            pl.BlockSpec((1, nb, width, MOBA_BLOCK), lambda bi, hg, i: (bi, 0, hg, 0)),
            pl.BlockSpec((1, nb, 1, width), lambda bi, hg, i: (bi, 0, 0, hg)),
        ],
        out_specs=pl.BlockSpec((1, MOBA_BLOCK, width), lambda bi, hg, i: (bi, i, hg)),
        out_shape=jax.ShapeDtypeStruct((b, s, d), BF16),
        scratch_shapes=[pltpu.VMEM((MOBA_HEADS, nb // MOBA_GROUP, MOBA_GROUP, MOBA_BLOCK), F32)]
                       + [pltpu.VMEM((MOBA_HEADS, MOBA_GROUP * MOBA_BLOCK, MOBA_BLOCK), BF16)] * 2
                       + [pltpu.VMEM((MOBA_HEADS, MOBA_GROUP, MOBA_BLOCK), F32)] * 2,
        compiler_params=_params(3),
        name="moba_attention",
    )(q, k, vt, km)


def _ffn_kernel(*refs, final, gated):
    refs = list(refs)
    x_ref, xh_ref, y_ref, yh_ref = refs[:4]
    del refs[:4]
    if gated:
        yg_ref, ygh_ref = refs[:2]
        del refs[:2]
    wo_ref, g_ref, wg_ref, wu_ref, cw_ref, cb_ref, wd_ref = refs[:7]
    del refs[:7]
    if final:
        gf_ref = refs.pop(0)
    o_ref, hn_ref, gx_ref, act_ref = refs
    i = pl.program_id(1)
    ts = x_ref.shape[1]
    nc, _, fc = wg_ref.shape
    halo = HALO_BF16

    def mixed(h_blk, y_blk, gate_blk):
        y = y_blk[0].astype(F32) * gate_blk[0] if gated else y_blk[0]
        return h_blk[0] + jnp.dot(y.astype(BF16), wo_ref[...], preferred_element_type=F32)

    x = mixed(x_ref, y_ref, yg_ref if gated else None)
    g = g_ref[...]
    hn_ref[0:halo, :] = _rms(mixed(xh_ref, yh_ref, ygh_ref if gated else None), g).astype(BF16)
    hn_ref[halo:, :] = _rms(x, g).astype(BF16)
    halo_keep = jnp.where(i == 0, 0.0, 1.0)

    for c in range(nc):
        gx = gx_ref.at[c % 2]
        gate = jnp.dot(hn_ref[...], wg_ref[c], preferred_element_type=F32)
        gx[0:halo, :] = gate[0:halo] * halo_keep
        gx[halo:, :] = gate[halo:]
        up = jnp.dot(hn_ref[halo:, :], wu_ref[c], preferred_element_type=F32)
        cw = cw_ref[c]
        conv = (gx[halo - 2:halo - 2 + ts, :] * cw[0:1]
                + gx[halo - 1:halo - 1 + ts, :] * cw[1:2]
                + gate[halo:] * cw[2:3]
                + cb_ref[c])
        act_ref[:, c * fc:(c + 1) * fc] = (conv * jax.nn.sigmoid(conv) * up).astype(BF16)

    out = x + jnp.dot(act_ref[...], wd_ref[...], preferred_element_type=F32)
    if final:
        out = _rms(out, gf_ref[...])
    o_ref[0] = out


def _ffn_call(h, y, y_gate, w_o, g, wg, wu, cw, cb, wd, g_final):
    b, s, d = h.shape
    ts = TOKEN_TILE
    nc, _, fc = wg.shape
    halo = HALO_BF16
    hpt = ts // halo
    final = g_final is not None
    gated = y_gate is not None
    tile = pl.BlockSpec((1, ts, d), lambda bi, i: (bi, i, 0))
    halo_blk = pl.BlockSpec((1, halo, d), lambda bi, i: (bi, jnp.maximum(i * hpt - 1, 0), 0))
    streams = [h, y] + ([y_gate] if gated else [])
    in_specs = [tile, halo_blk] * len(streams) + [
        _const_spec((d, d)),
        _const_spec((1, d)),
        _const_spec((nc, d, fc)),
        _const_spec((nc, d, fc)),
        _const_spec((nc, CONV_WIDTH, fc)),
        _const_spec((nc, 1, fc)),
        _const_spec((nc * fc, d)),
    ]
    args = [t for t in streams for _ in range(2)] + [w_o, g, wg, wu, cw, cb, wd]
    if final:
        in_specs.append(_const_spec((1, d)))
        args.append(g_final)
    return pl.pallas_call(
        functools.partial(_ffn_kernel, final=final, gated=gated),
        grid=(b, s // ts),
        in_specs=in_specs,
        out_specs=tile,
        out_shape=jax.ShapeDtypeStruct((b, s, d), F32),
        scratch_shapes=[
            pltpu.VMEM((ts + halo, d), BF16),
            pltpu.VMEM((2, ts + halo, fc), F32),
            pltpu.VMEM((ts, nc * fc), BF16),
        ],
        compiler_params=_params(2),
        name="conv_ffn_final" if final else "conv_ffn",
    )(*args)


def _rwkv_proj_kernel(x_ref, halo_ref, gm_ref, mu_ref, wr_ref, wk_ref, wv_ref,
                      w0_ref, w1_ref, w2_ref, a0_ref, a1_ref, a2_ref, g1_ref, g2_ref,
                      r_ref, k_ref, v_ref, lw_ref, a_ref, g_ref, xs_ref):
    i = pl.program_id(1)
    ts = x_ref.shape[1]
    halo = HALO_F32
    gm = gm_ref[...]
    xn = _rms(x_ref[0], gm)
    xs_ref[0:halo, :] = _rms(halo_ref[0], gm)
    xs_ref[halo:, :] = xn

    @pl.when(i == 0)
    def _():
        xs_ref[0:halo, :] = jnp.zeros((halo, xs_ref.shape[1]), F32)

    xx = xs_ref[halo - 1:halo - 1 + ts, :] - xn

    def mix(n):
        return (xn + xx * mu_ref[n:n + 1, :]).astype(BF16)

    r_ref[0] = jnp.dot(mix(0), wr_ref[...], preferred_element_type=F32)
    k_ref[0] = jnp.dot(mix(2), wk_ref[...], preferred_element_type=F32)
    v_ref[0] = jnp.dot(mix(3), wv_ref[...], preferred_element_type=F32)

    wl = w0_ref[...] + _mm(jnp.tanh(jnp.dot(mix(1), w1_ref[...], preferred_element_type=F32)),
                           w2_ref[...])
    z = -wl
    softplus = jnp.maximum(z, 0.0) + jnp.log(1.0 + jnp.exp(-jnp.abs(z)))
    w = -softplus - 0.5
    lw_ref[0] = -jnp.exp(w)

    al = a0_ref[...] + _mm(jnp.dot(mix(4), a1_ref[...], preferred_element_type=F32), a2_ref[...])
    a_ref[0] = jax.nn.sigmoid(al)
    g_ref[0] = _mm(jax.nn.sigmoid(jnp.dot(mix(5), g1_ref[...], preferred_element_type=F32)),
                   g2_ref[...])


def _rwkv_proj_call(h, gm, mu, wr, wk, wv, w0, w1, w2, a0, a1, a2, g1, g2):
    b, s, d = h.shape
    ts = RWKV_PROJ_TILE
    halo = HALO_F32
    hpt = ts // halo
    tile = pl.BlockSpec((1, ts, d), lambda bi, i: (bi, i, 0))
    consts = [gm, mu, wr, wk, wv, w0, w1, w2, a0, a1, a2, g1, g2]
    return pl.pallas_call(
        _rwkv_proj_kernel,
        grid=(b, s // ts),
        in_specs=[tile,
                  pl.BlockSpec((1, halo, d), lambda bi, i: (bi, jnp.maximum(i * hpt - 1, 0), 0))]
                 + [_const_spec(c.shape) for c in consts],
        out_specs=[tile] * 6,
        out_shape=[jax.ShapeDtypeStruct((b, s, d), F32)] * 6,
        scratch_shapes=[pltpu.VMEM((ts + halo, d), F32)],
        compiler_params=_params(2),
        name="rwkv_proj",
    )(h, h, *consts)


def _rwkv_core_kernel(r_ref, k_ref, v_ref, lw_ref, a_ref, kk_ref, ka_ref, rk_ref,
                      lnw_ref, lnb_ref, o_ref, h_ref):
    c = RWKV_CHUNK
    hd = HEAD_DIM
    c_shift = int(math.log2(c))
    step = r_ref.shape[1]
    nch = step // c

    @pl.when(pl.program_id(2) == 0)
    def _():
        h_ref[...] = jnp.zeros_like(h_ref)

    lane = lax.broadcasted_iota(jnp.int32, (1, LANES), 1)
    head0 = lane < hd

    def by_head(x0, x1):
        return jnp.where(head0, x0, x1)

    def seg_sum(t):
        s0 = jnp.sum(jnp.where(head0, t, 0.0), axis=-1, keepdims=True)
        s1 = jnp.sum(jnp.where(head0, 0.0, t), axis=-1, keepdims=True)
        return by_head(s0, s1)

    r, k, v, lw, a = r_ref[0], k_ref[0], v_ref[0], lw_ref[0], a_ref[0]

    sub = min(step, RWKV_CUMSUM_ROWS)
    tr = lax.broadcasted_iota(jnp.int32, (sub, sub), 0)
    tc = lax.broadcasted_iota(jnp.int32, (sub, sub), 1)
    same_chunk = lax.shift_right_logical(tr, c_shift) == lax.shift_right_logical(tc, c_shift)
    ltri = jnp.where((tr >= tc) & same_chunk, 1.0, 0.0).astype(BF16)
    hi = lw.astype(BF16)
    rem = lw - hi.astype(F32)
    mid = rem.astype(BF16)
    lo = (rem - mid.astype(F32)).astype(BF16)
    split = jnp.concatenate([hi, mid, lo], axis=1)
    cum3 = jnp.concatenate(
        [jnp.dot(ltri, split[n * sub:(n + 1) * sub], preferred_element_type=F32)
         for n in range(step // sub)], axis=0)
    cum = cum3[:, :LANES] + cum3[:, LANES:2 * LANES] + cum3[:, 2 * LANES:]

    e_w = jnp.exp(cum)
    e_wi = jnp.exp(-cum)
    e_wp = jnp.exp(cum - lw)
    kk = k * kk_ref[...]
    kk = kk / jnp.maximum(jnp.sqrt(seg_sum(kk * kk)), 1e-12)
    kmod = k * (1.0 + (a - 1.0) * ka_ref[...])
    a_t = -kk * e_wp
    b_t = kk * a * e_wi
    k_t = kmod * e_wi
    r_t = r * e_w

    rows = [slice(ch * c, (ch + 1) * c) for ch in range(nch)]
    w_c = [e_w[ch * c + c - 1:ch * c + c, :] for ch in range(nch)]
    zeros_c = jnp.zeros((c, LANES), F32)

    r4 = lax.broadcasted_iota(jnp.int32, (4 * c, LANES), 0)
    l4 = lax.broadcasted_iota(jnp.int32, (4 * c, LANES), 1)
    t_idx = r4 & (c - 1)
    s_idx = l4 & (c - 1)
    incl = lax.shift_right_logical(r4, c_shift) & 1
    tri_mask = t_idx - s_idx + incl > 0
    r2 = lax.broadcasted_iota(jnp.int32, (LANES, LANES), 0)
    l2 = lax.broadcasted_iota(jnp.int32, (LANES, LANES), 1)
    eye = r2 == l2
    blockdiag = (r2 < hd) == (l2 < hd)
    eye_f = jnp.where(eye, 1.0, 0.0)

    aa, rr = [], []
    for rw in rows:
        a_c, r_c = a_t[rw], r_t[rw]
        lhs = jnp.concatenate([by_head(a_c, 0.0), by_head(r_c, 0.0),
                               by_head(0.0, a_c), by_head(0.0, r_c)], axis=0)
        rhs = jnp.concatenate([b_t[rw], k_t[rw]], axis=0)
        res = jnp.where(tri_mask, _mm_nt(lhs, rhs), 0.0)
        aa.append(jnp.concatenate([res[0:c], res[2 * c:3 * c]], axis=0))
        rr.append(jnp.concatenate([res[c:2 * c], res[3 * c:4 * c]], axis=0))

    rp, y0, gg, dd = ([None] * nch for _ in range(4))

    def parallel_stages(chunks, tick):
        rws = [rows[ch] for ch in chunks]
        wcs = [w_c[ch] for ch in chunks]
        aas = [aa[ch] for ch in chunks]
        n_bd = [jnp.where(blockdiag, by_head(x, pltpu.roll(x, hd, 1)), 0.0) for x in aas]
        inv = [eye_f + n for n in n_bd]
        p = n_bd
        for _ in range(c_shift - 1):
            p = [_mm(x, x) for x in p]
            tick()
            inv = [t + _mm(t, x) for t, x in zip(inv, p)]
            tick()
        akv = [_mm(x, jnp.concatenate([zeros_c, v[rw]], axis=0)) for x, rw in zip(aas, rws)]
        tick()
        ua = [_mm(t, jnp.concatenate([x, jnp.concatenate([a_t[rw], a_t[rw]], axis=0)], axis=1))
              for t, x, rw in zip(inv, akv, rws)]
        tick()
        u0 = [by_head(x[0:c, :LANES], x[c:2 * c, :LANES]) for x in ua]
        a2 = [by_head(x[0:c, LANES:], x[c:2 * c, LANES:]) for x in ua]
        rhs4 = [jnp.concatenate([jnp.concatenate([x, zeros_c], axis=0),
                                 jnp.concatenate([u, v[rw]], axis=0)], axis=1)
                for x, u, rw in zip(a2, u0, rws)]
        res4 = [_mm(rr[ch], y) for ch, y in zip(chunks, rhs4)]
        tick()
        gd = [_mm_tn(jnp.concatenate([b_t[rw] * w, k_t[rw] * w], axis=0), y)
              for rw, w, y in zip(rws, wcs, rhs4)]
        tick()
        for n, ch in enumerate(chunks):
            x = res4[n]
            rp[ch] = r_t[rws[n]] + by_head(x[0:c, :LANES], x[c:2 * c, :LANES])
            y0[ch] = by_head(x[0:c, LANES:], x[c:2 * c, LANES:])
            gg[ch] = jnp.where(blockdiag, gd[n][:, :LANES], 0.0) + jnp.where(eye, wcs[n], 0.0)
            dd[ch] = jnp.where(blockdiag, gd[n][:, LANES:], 0.0)

    chain = {"hh": h_ref[...], "todo": []}
    ys = [None] * nch

    def chain_step():
        if chain["todo"]:
            ch = chain["todo"].pop(0)
            ys[ch] = _mm(rp[ch], chain["hh"]) + y0[ch]
            chain["hh"] = _mm(gg[ch], chain["hh"]) + dd[ch]

    per_part = nch // RWKV_PARTS
    parts = [list(range(n * per_part, (n + 1) * per_part)) for n in range(RWKV_PARTS)]
    parallel_stages(parts[0], lambda: None)
    for prev, part in zip(parts[:-1], parts[1:]):
        chain["todo"] += prev
        parallel_stages(part, chain_step)
    chain["todo"] += parts[-1]
    while chain["todo"]:
        chain_step()
    h_ref[...] = chain["hh"]
    y = jnp.concatenate(ys, axis=0)

    mean = seg_sum(y) * (1.0 / hd)
    dev = y - mean
    var = seg_sum(dev * dev) * (1.0 / hd)
    yn = dev * lax.rsqrt(var + LNX_EPS) * lnw_ref[...] + lnb_ref[...]
    bonus = seg_sum(r * kmod * rk_ref[...]) * v
    o_ref[0] = yn + bonus


def _rwkv_core_call(r, k, v, lw, a, k_k, k_a, r_k, ln_w, ln_b):
    b, s, d = r.shape
    step = RWKV_STEP
    tile = pl.BlockSpec((1, step, LANES), lambda bi, hp, i: (bi, i, hp))
    par = pl.BlockSpec((1, LANES), lambda bi, hp, i: (0, hp))
    return pl.pallas_call(
        _rwkv_core_kernel,
        grid=(b, d // LANES, s // step),
        in_specs=[tile] * 5 + [par] * 5,
        out_specs=tile,
        out_shape=jax.ShapeDtypeStruct((b, s, d), F32),
        scratch_shapes=[pltpu.VMEM((LANES, LANES), F32)],
        compiler_params=_params(3),
        name="rwkv_recurrence",
    )(r, k, v, lw, a, k_k, k_a, r_k, ln_w, ln_b)


def _rope_tables(seq):
    inv = 1.0 / (ROPE_THETA ** (jnp.arange(0, HEAD_DIM, 2, dtype=F32) / HEAD_DIM))
    ang = jnp.arange(seq, dtype=F32)[:, None] * inv[None, :]
    cos, sin = jnp.cos(ang), jnp.sin(ang)
    cos = jnp.concatenate([cos, cos] * HEADS_PER_PAIR, axis=-1)
    sin = jnp.concatenate([-sin, sin] * HEADS_PER_PAIR, axis=-1)
    return cos, sin


def _ffn_weights(w_gate, w_up, conv_w, conv_b, w_down):
    d, f = w_gate.shape
    fc = FFN_FCHUNK
    nc = f // fc
    wg = w_gate.astype(BF16).reshape(d, nc, fc).transpose(1, 0, 2)
    wu = w_up.astype(BF16).reshape(d, nc, fc).transpose(1, 0, 2)
    cw = conv_w.reshape(CONV_WIDTH, nc, fc).transpose(1, 0, 2)
    cb = conv_b.reshape(nc, 1, fc)
    wd = w_down.astype(BF16)
    return wg, wu, cw, cb, wd


def kernel(x, norm_mix, norm_ffn, norm_final, attn_w_qkv, attn_w_o, rwkv_mu, rwkv_w_rkv, rwkv_w0, rwkv_w1, rwkv_w2, rwkv_a0, rwkv_a1, rwkv_a2, rwkv_g1, rwkv_g2, rwkv_k_k, rwkv_k_a, rwkv_r_k, rwkv_lnx_w, rwkv_lnx_b, rwkv_w_o, ffn_w_gate, ffn_w_up, ffn_conv_w, ffn_conv_b, ffn_w_down):
    b, s, d = x.shape
    depth = norm_mix.shape[0]
    n_mixers = 2
    row = lambda t: t.reshape(1, d)
    cos, sin = _rope_tables(s)
    h = x
    for i in range(depth):
        j = i // n_mixers
        if i % n_mixers == 0:
            w_qkv = attn_w_qkv[j].astype(BF16)
            q, k, vt, km = _qkv_call(h, row(norm_mix[i]), w_qkv[:, :d], w_qkv[:, d:2 * d],
                                     w_qkv[:, 2 * d:].T, cos, sin)
            y, y_gate, w_o = _moba_call(q, k, vt, km), None, attn_w_o[j].astype(BF16)
        else:
            bf = lambda t: t.astype(BF16)
            r, k, v, lw, a, y_gate = _rwkv_proj_call(
                h, row(norm_mix[i]), rwkv_mu[j], bf(rwkv_w_rkv[j, 0]), bf(rwkv_w_rkv[j, 1]),
                bf(rwkv_w_rkv[j, 2]), row(rwkv_w0[j]), bf(rwkv_w1[j]), bf(rwkv_w2[j]),
                row(rwkv_a0[j]), bf(rwkv_a1[j]), bf(rwkv_a2[j]), bf(rwkv_g1[j]), bf(rwkv_g2[j]))
            y = _rwkv_core_call(r, k, v, lw, a, row(rwkv_k_k[j]), row(rwkv_k_a[j]),
                                row(rwkv_r_k[j]), row(rwkv_lnx_w[j]), row(rwkv_lnx_b[j]))
            w_o = bf(rwkv_w_o[j])
        g_final = row(norm_final) if i == depth - 1 else None
        h = _ffn_call(h, y, y_gate, w_o, row(norm_ffn[i]),
                      *_ffn_weights(ffn_w_gate[i], ffn_w_up[i], ffn_conv_w[i], ffn_conv_b[i],
                                    ffn_w_down[i]), g_final)
    return h
```
